```python
import jax, jax.numpy as jnp
from jax import lax
import numpy as np

D_MODEL = 2048
BATCH = 1
SEQ = 8192
DEPTH = 2

GRID_W = 64
GM_GROUPS = 8
GM_HEAD = 128
GM_WIDTH = GM_GROUPS * GM_HEAD
CHUNK = 128
NA_HEADS = 16
NA_HEAD_DIM = 64
NA_WIDTH = NA_HEADS * NA_HEAD_DIM
NA_KR = 8
NA_KC = 16
N_BRANCH = 2
IN_COLS = 2 * GM_WIDTH + 3 * NA_WIDTH + N_BRANCH * D_MODEL
D_FF = 5504
CONV_W = 3
RMS_EPS = 1e-6
LN_EPS = 1e-5
NEG_INF = -1e30

kernel_name = "hybrid_gmlp_natten_convffn_encoder"


def rms_norm(x, g):
    xf = x.astype(jnp.float32)
    y = xf * lax.rsqrt(jnp.mean(xf * xf, axis=-1, keepdims=True) + RMS_EPS)
    return (y * g.astype(jnp.float32)).astype(x.dtype)


def gmlp_branch(uv, ln_g, ln_b, w_s, b_s):
    b, t, _ = uv.shape
    uv = jax.nn.gelu(uv)
    u, v = jnp.split(uv, 2, axis=-1)
    vf = v.astype(jnp.float32)
    mu = jnp.mean(vf, axis=-1, keepdims=True)
    var = jnp.mean(jnp.square(vf - mu), axis=-1, keepdims=True)
    v = ((vf - mu) * lax.rsqrt(var + LN_EPS) * ln_g.astype(jnp.float32)
         + ln_b.astype(jnp.float32)).astype(u.dtype)
    v = v.reshape(b, t // CHUNK, CHUNK, GM_GROUPS, GM_HEAD)
    mixed = jnp.einsum('gpq,bnqgc->bnpgc', w_s, v) + b_s.T[None, None, :, :, None]
    return u * mixed.reshape(b, t, GM_WIDTH)


def neighbourhood_attn(q, k, v, rpb):
    b, t, _ = q.shape
    rows = t // GRID_W
    kr = min(NA_KR, rows)

    def to_grid(z):
        return z.reshape(b, rows, GRID_W, NA_HEADS, NA_HEAD_DIM).transpose(0, 3, 1, 2, 4)

    qg, kg, vg = to_grid(q), to_grid(k), to_grid(v)
    r = jnp.arange(rows)
    row_start = jnp.clip(r - kr // 2, 0, rows - kr)
    row_idx = row_start[:, None] + jnp.arange(kr)[None, :]
    k_blk = kg[:, :, row_idx]
    v_blk = vg[:, :, row_idx]

    c = jnp.arange(GRID_W)
    col_start = jnp.clip(c - NA_KC // 2, 0, GRID_W - NA_KC)
    in_win = (c[None, :] >= col_start[:, None]) & (c[None, :] < col_start[:, None] + NA_KC)
    dr_i = (row_idx - r[:, None]) + (NA_KR - 1)
    dc_i = jnp.clip(c[None, :] - c[:, None], -(NA_KC - 1), NA_KC - 1) + (NA_KC - 1)
    bias = rpb.astype(jnp.float32)[:, dr_i[:, None, :, None], dc_i[None, :, None, :]]
    bias = jnp.where(in_win[None, None, :, None, :], bias, NEG_INF)

    scale = NA_HEAD_DIM ** -0.5
    s = jnp.einsum('bhrcd,bhrikd->bhrcik', qg.astype(jnp.float32),
                   k_blk.astype(jnp.float32)) * scale + bias[None]
    p = jax.nn.softmax(s.reshape(b, NA_HEADS, rows, GRID_W, kr * GRID_W), axis=-1)
    p = p.reshape(b, NA_HEADS, rows, GRID_W, kr, GRID_W)
    o = jnp.einsum('bhrcik,bhrikd->bhrcd', p, v_blk.astype(jnp.float32))
    return o.transpose(0, 2, 3, 1, 4).reshape(b, t, NA_WIDTH).astype(q.dtype)


def mixer_block(x, g_norm, w_in, ln_g, ln_b, w_s, b_s, rpb, w_branch, w_out):
    h = rms_norm(x, g_norm)
    proj = h @ w_in
    uv, qkv, gates = jnp.split(proj, [2 * GM_WIDTH, 2 * GM_WIDTH + 3 * NA_WIDTH], axis=-1)
    q, k, v = jnp.split(qkv, 3, axis=-1)
    y_a = gmlp_branch(uv, ln_g, ln_b, w_s, b_s)
    y_b = neighbourhood_attn(q, k, v, rpb)
    g_a, g_b = jnp.split(jax.nn.sigmoid(gates), 2, axis=-1)
    merged = g_a * (y_a @ w_branch[:GM_WIDTH]) + g_b * (y_b @ w_branch[GM_WIDTH:])
    return x + merged @ w_out


def conv_ffn(x, g_norm, w_up, conv_k, conv_b, w_down):
    h = rms_norm(x, g_norm)
    up = h @ w_up
    t = up.shape[1]
    pad = jnp.pad(up, ((0, 0), (1, 1), (0, 0)))
    up = (pad[:, :t] * conv_k[0] + pad[:, 1:t + 1] * conv_k[1]
          + pad[:, 2:t + 2] * conv_k[2] + conv_b)
    gate, val = jnp.split(up, 2, axis=-1)
    return x + (jax.nn.silu(gate) * val) @ w_down


def setup_inputs(seed: int = 0) -> dict:
    key = jax.random.key(seed)
    ks = jax.random.split(key, 17)
    f32 = jnp.float32
    n = lambda k, s: jax.random.normal(k, s, f32)
    return {
        "x": n(ks[0], (BATCH, SEQ, D_MODEL)),
        "norm_mix": 1.0 + 0.1 * n(ks[1], (DEPTH, D_MODEL)),
        "w_in": n(ks[2], (DEPTH, D_MODEL, IN_COLS)) * D_MODEL ** -0.5,
        "gm_ln_g": 1.0 + 0.1 * n(ks[3], (DEPTH, GM_WIDTH)),
        "gm_ln_b": 0.01 * n(ks[4], (DEPTH, GM_WIDTH)),
        "gm_w_s": n(ks[5], (DEPTH, GM_GROUPS, CHUNK, CHUNK)) * CHUNK ** -0.5,
        "gm_b_s": 1.0 + 0.1 * n(ks[6], (DEPTH, GM_GROUPS, CHUNK)),
        "na_rpb": 0.5 * n(ks[7], (DEPTH, NA_HEADS, 2 * NA_KR - 1, 2 * NA_KC - 1)),
        "w_branch": n(ks[8], (DEPTH, GM_WIDTH + NA_WIDTH, D_MODEL)) * GM_WIDTH ** -0.5,
        "w_out": n(ks[9], (DEPTH, D_MODEL, D_MODEL)) * D_MODEL ** -0.5,
        "norm_ffn": 1.0 + 0.1 * n(ks[10], (DEPTH, D_MODEL)),
        "w_up": n(ks[11], (DEPTH, D_MODEL, 2 * D_FF)) * D_MODEL ** -0.5,
        "conv_k": n(ks[12], (DEPTH, CONV_W, 2 * D_FF)) * CONV_W ** -0.5,
        "conv_b": 0.01 * n(ks[13], (DEPTH, 2 * D_FF)),
        "w_down": n(ks[14], (DEPTH, D_FF, D_MODEL)) * D_FF ** -0.5,
        "norm_final": 1.0 + 0.1 * n(ks[15], (D_MODEL,)),
    }


def reference(x, norm_mix, w_in, gm_ln_g, gm_ln_b, gm_w_s, gm_b_s, na_rpb, w_branch, w_out,
              norm_ffn, w_up, conv_k, conv_b, w_down, norm_final):
    for l in range(DEPTH):
        x = mixer_block(x, norm_mix[l], w_in[l], gm_ln_g[l], gm_ln_b[l], gm_w_s[l], gm_b_s[l],
                        na_rpb[l], w_branch[l], w_out[l])
        x = conv_ffn(x, norm_ffn[l], w_up[l], conv_k[l], conv_b[l], w_down[l])
    return rms_norm(x, norm_final)
```

```python
from functools import partial

import jax
import jax.numpy as jnp
from jax import lax
from jax.experimental import pallas as pl
from jax.experimental.pallas import tpu as pltpu

D_MODEL = 2048
SEQ = 8192
DEPTH = 2
GRID_W = 64
ROWS = SEQ // GRID_W
GM_GROUPS = 8
GM_HEAD = 128
GM_WIDTH = GM_GROUPS * GM_HEAD
CHUNK = 128
NA_HEADS = 16
NA_HEAD_DIM = 64
NA_WIDTH = NA_HEADS * NA_HEAD_DIM
NA_KR = 8
NA_KC = 16
D_FF = 5504
RMS_EPS = 1e-6
LN_EPS = 1e-5
NEG_INF = -1e30

LANES = 128
SUBLANES = 8
VMEM_LIMIT_BYTES = 56 * 1024 * 1024

FF_TILE = 512
D_FF_PAD = ((D_FF + FF_TILE - 1) // FF_TILE) * FF_TILE
N_FF_TILES = D_FF_PAD // FF_TILE
PROJ_TM = 1024
PROJ_TN = 1024
GMLP_TM = 512
MERGE_TM = 256
FFN_TM = 512
HALO = SUBLANES
NA_QROWS = 2
NA_SLOTS = NA_KR + NA_QROWS - 1
NA_QTOK = NA_QROWS * GRID_W
NA_KTOK = NA_SLOTS * GRID_W
NA_ROWS_PER_STEP = 16
NA_PAIRS_PER_STEP = NA_ROWS_PER_STEP // NA_QROWS
NA_CLASS_R0 = (ROWS // 2, 0, 2, ROWS - 4, ROWS - 2)
N_CLASSES = len(NA_CLASS_R0)

_BF16 = jnp.bfloat16
_F32 = jnp.float32


def _params(sem):
    return pltpu.CompilerParams(dimension_semantics=sem, vmem_limit_bytes=VMEM_LIMIT_BYTES)


def _rms(x, g):
    return x * lax.rsqrt(jnp.mean(x * x, axis=-1, keepdims=True) + RMS_EPS) * g


def _gelu_tanh(x):
    return 0.5 * x * (1.0 + jnp.tanh(0.7978845608028654 * (x + 0.044715 * (x * x * x))))


def _sigmoid(x):
    return 1.0 / (1.0 + jnp.exp(-x))


_ACTS = {"gelu": _gelu_tanh, "sigmoid": _sigmoid, "none": lambda x: x}


def _norm_proj_kernel(x_ref, g_ref, w_ref, o_ref, h_scr, *, act):
    @pl.when(pl.program_id(1) == 0)
    def _():
        h_scr[...] = _rms(x_ref[...], g_ref[...]).astype(_BF16)

    acc = jnp.dot(h_scr[...], w_ref[...], preferred_element_type=_F32)
    o_ref[...] = _ACTS[act](acc).astype(o_ref.dtype)


def _norm_proj(x, g, w, col0, ncols, act, name):
    m = x.shape[0]
    jb0 = col0 // PROJ_TN
    return pl.pallas_call(
        partial(_norm_proj_kernel, act=act),
        grid=(m // PROJ_TM, ncols // PROJ_TN),
        in_specs=[
            pl.BlockSpec((PROJ_TM, D_MODEL), lambda i, j: (i, 0)),
            pl.BlockSpec((1, D_MODEL), lambda i, j: (0, 0)),
            pl.BlockSpec((D_MODEL, PROJ_TN), lambda i, j: (0, jb0 + j)),
        ],
        out_specs=pl.BlockSpec((PROJ_TM, PROJ_TN), lambda i, j: (i, j)),
        out_shape=jax.ShapeDtypeStruct((m, ncols), _BF16),
        scratch_shapes=[pltpu.VMEM((PROJ_TM, D_MODEL), _BF16)],
        compiler_params=_params(("arbitrary", "arbitrary")),
        name=name,
    )(x, g, w)


def _gmlp_kernel(uv_ref, lng_ref, lnb_ref, ws_ref, bs_ref, o_ref):
    v = uv_ref[:, GM_WIDTH:].astype(_F32)
    mu = jnp.mean(v, axis=-1, keepdims=True)
    vc = v - mu
    var = jnp.mean(vc * vc, axis=-1, keepdims=True)
    vn = (vc * lax.rsqrt(var + LN_EPS) * lng_ref[...] + lnb_ref[...]).astype(_BF16)
    for n in range(GMLP_TM // CHUNK):
        rows = slice(n * CHUNK, (n + 1) * CHUNK)
        for g in range(GM_GROUPS):
            cols = slice(g * GM_HEAD, (g + 1) * GM_HEAD)
            mixed = jnp.dot(ws_ref[g], vn[rows, cols], preferred_element_type=_F32) + bs_ref[g]
            o_ref[rows, cols] = (uv_ref[rows, cols].astype(_F32) * mixed).astype(o_ref.dtype)


def _gmlp(uv, ln_g, ln_b, w_s, b_s):
    m = uv.shape[0]
    return pl.pallas_call(
        _gmlp_kernel,
        grid=(m // GMLP_TM,),
        in_specs=[
            pl.BlockSpec((GMLP_TM, 2 * GM_WIDTH), lambda i: (i, 0)),
            pl.BlockSpec((1, GM_WIDTH), lambda i: (0, 0)),
            pl.BlockSpec((1, GM_WIDTH), lambda i: (0, 0)),
            pl.BlockSpec((GM_GROUPS, CHUNK, CHUNK), lambda i: (0, 0, 0)),
            pl.BlockSpec((GM_GROUPS, CHUNK, 1), lambda i: (0, 0, 0)),
        ],
        out_specs=pl.BlockSpec((GMLP_TM, GM_WIDTH), lambda i: (i, 0)),
        out_shape=jax.ShapeDtypeStruct((m, GM_WIDTH), _BF16),
        compiler_params=_params(("arbitrary",)),
        name="gmlp_gate",
    )(uv, ln_g, ln_b, w_s, b_s)


def _pair_window(r0):
    base = min(max(r0 - NA_KR // 2, 0), ROWS - NA_SLOTS)
    table = []
    for qrow in range(NA_QROWS):
        r = r0 + qrow
        rs = min(max(r - NA_KR // 2, 0), ROWS - NA_KR)
        slots = []
        for s in range(NA_SLOTS):
            krow = base + s
            slots.append(krow - r + (NA_KR - 1) if rs <= krow < rs + NA_KR else None)
        table.append(slots)
    return base, table


def _na_bias_kernel(rpb_ref, o_ref):
    lh = pl.program_id(0)
    n_dr, n_dc = 2 * NA_KR - 1, 2 * NA_KC - 1
    c = lax.broadcasted_iota(jnp.int32, (GRID_W, GRID_W), 0)
    k = lax.broadcasted_iota(jnp.int32, (GRID_W, GRID_W), 1)
    cs = jnp.clip(c - NA_KC // 2, 0, GRID_W - NA_KC)
    in_win = (k >= cs) & (k < cs + NA_KC)
    dc = jnp.clip(k - c, -(NA_KC - 1), NA_KC - 1) + (NA_KC - 1)
    dc_eq = [dc == d for d in range(n_dc)]
    neg = jnp.full((GRID_W, GRID_W), NEG_INF, _F32)
    tiles = []
    for dr in range(n_dr):
        t = neg
        for d in range(n_dc):
            t = jnp.where(dc_eq[d], rpb_ref[(lh * n_dr + dr) * n_dc + d], t)
        tiles.append(jnp.where(in_win, t, neg))
    for cls, r0 in enumerate(NA_CLASS_R0):
        _, table = _pair_window(r0)
        for qrow in range(NA_QROWS):
            for s in range(NA_SLOTS):
                dr = table[qrow][s]
                o_ref[0, cls, qrow * GRID_W:(qrow + 1) * GRID_W, s * GRID_W:(s + 1) * GRID_W] = (
                    neg if dr is None else tiles[dr])


def _na_bias(rpb_flat):
    n = DEPTH * NA_HEADS
    return pl.pallas_call(
        _na_bias_kernel,
        grid=(n,),
        in_specs=[pl.BlockSpec(memory_space=pltpu.SMEM)],
        out_specs=pl.BlockSpec((1, N_CLASSES, NA_QTOK, NA_KTOK), lambda i: (i, 0, 0, 0)),
        out_shape=jax.ShapeDtypeStruct((n, N_CLASSES, NA_QTOK, NA_KTOK), _F32),
        compiler_params=_params(("arbitrary",)),
        name="na_bias",
    )(rpb_flat)


def _na_kernel(q_ref, k_ref, v_ref, b_ref, o_ref):
    rb = pl.program_id(1)
    scale = NA_HEAD_DIM ** -0.5

    def pair(t, carry):
        r0 = rb * NA_ROWS_PER_STEP + t * NA_QROWS
        base = jnp.clip(r0 - NA_KR // 2, 0, ROWS - NA_SLOTS)
        cls = jnp.where(r0 == NA_CLASS_R0[1], 1,
              jnp.where(r0 == NA_CLASS_R0[2], 2,
              jnp.where(r0 == NA_CLASS_R0[3], 3,
              jnp.where(r0 == NA_CLASS_R0[4], 4, 0))))
        qrows = pl.ds(pl.multiple_of(t * NA_QTOK, NA_QTOK), NA_QTOK)
        krows = pl.ds(pl.multiple_of(base * GRID_W, GRID_W), NA_KTOK)
        q2 = q_ref[qrows, :]
        k2 = k_ref[krows, :]
        v2 = v_ref[krows, :]
        outs = []
        for hh in range(2):
            cols = slice(hh * NA_HEAD_DIM, (hh + 1) * NA_HEAD_DIM)
            s = lax.dot_general(q2[:, cols], k2[:, cols], (((1,), (1,)), ((), ())),
                                preferred_element_type=_F32)
            s = s * scale + b_ref[hh, cls]
            p = jnp.exp(s - jnp.max(s, axis=-1, keepdims=True))
            den = jnp.sum(p, axis=-1, keepdims=True)
            o = jnp.dot(p.astype(_BF16), v2[:, cols], preferred_element_type=_F32)
            outs.append(o / den)
        o_ref[qrows, :] = jnp.concatenate(outs, axis=-1).astype(o_ref.dtype)
        return carry

    lax.fori_loop(0, NA_PAIRS_PER_STEP, pair, 0)


def _na(qkv, bias, layer):
    m = qkv.shape[0]
    tq = NA_ROWS_PER_STEP * GRID_W
    n_hp = NA_WIDTH // LANES
    return pl.pallas_call(
        _na_kernel,
        grid=(n_hp, m // tq),
        in_specs=[
            pl.BlockSpec((tq, LANES), lambda hp, rb: (rb, hp)),
            pl.BlockSpec((m, LANES), lambda hp, rb: (0, n_hp + hp)),
            pl.BlockSpec((m, LANES), lambda hp, rb: (0, 2 * n_hp + hp)),
            pl.BlockSpec((2, N_CLASSES, NA_QTOK, NA_KTOK),
                         lambda hp, rb: (layer * n_hp + hp, 0, 0, 0)),
        ],
        out_specs=pl.BlockSpec((tq, LANES), lambda hp, rb: (rb, hp)),
        out_shape=jax.ShapeDtypeStruct((m, NA_WIDTH), _BF16),
        compiler_params=_params(("arbitrary", "arbitrary")),
        name="na_attn",
    )(qkv, qkv, qkv, bias)


def _merge_kernel(x_ref, ya_ref, yb_ref, gt_ref, wb_ref, wo_ref, o_ref):
    ta = jnp.dot(ya_ref[...], wb_ref[:GM_WIDTH, :], preferred_element_type=_F32)
    tb = jnp.dot(yb_ref[...], wb_ref[GM_WIDTH:, :], preferred_element_type=_F32)
    merged = (gt_ref[:, :D_MODEL].astype(_F32) * ta + gt_ref[:, D_MODEL:].astype(_F32) * tb)
    o_ref[...] = x_ref[...] + jnp.dot(merged.astype(_BF16), wo_ref[...],
                                      preferred_element_type=_F32)


def _merge(x, ya, yb, gates, w_branch, w_out):
    m = x.shape[0]
    resident = dict(pipeline_mode=pl.Buffered(1))
    return pl.pallas_call(
        _merge_kernel,
        grid=(m // MERGE_TM,),
        in_specs=[
            pl.BlockSpec((MERGE_TM, D_MODEL), lambda i: (i, 0)),
            pl.BlockSpec((MERGE_TM, GM_WIDTH), lambda i: (i, 0)),
            pl.BlockSpec((MERGE_TM, NA_WIDTH), lambda i: (i, 0)),
            pl.BlockSpec((MERGE_TM, 2 * D_MODEL), lambda i: (i, 0)),
            pl.BlockSpec((GM_WIDTH + NA_WIDTH, D_MODEL), lambda i: (0, 0), **resident),
            pl.BlockSpec((D_MODEL, D_MODEL), lambda i: (0, 0), **resident),
        ],
        out_specs=pl.BlockSpec((MERGE_TM, D_MODEL), lambda i: (i, 0)),
        out_shape=jax.ShapeDtypeStruct((m, D_MODEL), _F32),
        compiler_params=_params(("arbitrary",)),
        name="merge_out",
    )(x, ya, yb, gates, w_branch, w_out)


def _ffn_kernel(x_ref, xp_ref, xn_ref, g_ref, wg_ref, wv_ref, kg_ref, kv_ref, bg_ref, bv_ref,
                wd_ref, gf_ref, o_ref, h_scr, ug_scr, uv_scr, *, final_norm):
    i, j = pl.program_id(0), pl.program_id(1)
    tm = FFN_TM

    @pl.when(j == 0)
    def _():
        g = g_ref[...]
        keep_prev = (i > 0).astype(_F32)
        keep_next = (i < pl.num_programs(0) - 1).astype(_F32)
        h_scr[:HALO, :] = (_rms(xp_ref[...], g) * keep_prev).astype(_BF16)
        h_scr[HALO:HALO + tm, :] = _rms(x_ref[...], g).astype(_BF16)
        h_scr[HALO + tm:, :] = (_rms(xn_ref[...], g) * keep_next).astype(_BF16)
        o_ref[...] = x_ref[...]

    h = h_scr[...]
    ug_scr[...] = jnp.dot(h, wg_ref[...], preferred_element_type=_F32)
    uv_scr[...] = jnp.dot(h, wv_ref[...], preferred_element_type=_F32)

    def conv(u_scr, k_ref, b_ref):
        return (u_scr[HALO - 1:HALO - 1 + tm, :] * k_ref[0:1, :]
                + u_scr[HALO:HALO + tm, :] * k_ref[1:2, :]
                + u_scr[HALO + 1:HALO + 1 + tm, :] * k_ref[2:3, :] + b_ref[...])

    gate = conv(ug_scr, kg_ref, bg_ref)
    val = conv(uv_scr, kv_ref, bv_ref)
    act = (gate * _sigmoid(gate) * val).astype(_BF16)
    o_ref[...] += jnp.dot(act, wd_ref[...], preferred_element_type=_F32)

    if final_norm:
        @pl.when(j == pl.num_programs(1) - 1)
        def _():
            o_ref[...] = _rms(o_ref[...], gf_ref[...])


def _ffn(x, g, w_up, conv_k, conv_b, w_down, g_final, final_norm):
    m = x.shape[0]
    tm = FFN_TM
    nb = tm // HALO
    last = m // HALO - 1
    return pl.pallas_call(
        partial(_ffn_kernel, final_norm=final_norm),
        grid=(m // tm, N_FF_TILES),
        in_specs=[
            pl.BlockSpec((tm, D_MODEL), lambda i, j: (i, 0)),
            pl.BlockSpec((HALO, D_MODEL), lambda i, j: (jnp.maximum(i * nb - 1, 0), 0)),
            pl.BlockSpec((HALO, D_MODEL), lambda i, j: (jnp.minimum((i + 1) * nb, last), 0)),
            pl.BlockSpec((1, D_MODEL), lambda i, j: (0, 0)),
            pl.BlockSpec((D_MODEL, FF_TILE), lambda i, j: (0, j)),
            pl.BlockSpec((D_MODEL, FF_TILE), lambda i, j: (0, N_FF_TILES + j)),
            pl.BlockSpec((3, FF_TILE), lambda i, j: (0, j)),
            pl.BlockSpec((3, FF_TILE), lambda i, j: (0, N_FF_TILES + j)),
            pl.BlockSpec((1, FF_TILE), lambda i, j: (0, j)),
            pl.BlockSpec((1, FF_TILE), lambda i, j: (0, N_FF_TILES + j)),
            pl.BlockSpec((FF_TILE, D_MODEL), lambda i, j: (j, 0)),
            pl.BlockSpec((1, D_MODEL), lambda i, j: (0, 0)),
        ],
        out_specs=pl.BlockSpec((tm, D_MODEL), lambda i, j: (i, 0)),
        out_shape=jax.ShapeDtypeStruct((m, D_MODEL), _F32),
        scratch_shapes=[
            pltpu.VMEM((tm + 2 * HALO, D_MODEL), _BF16),
            pltpu.VMEM((tm + 2 * HALO, FF_TILE), _F32),
            pltpu.VMEM((tm + 2 * HALO, FF_TILE), _F32),
        ],
        compiler_params=_params(("arbitrary", "arbitrary")),
        name="conv_ffn",
    )(x, x, x, g, w_up, w_up, conv_k, conv_k, conv_b, conv_b, w_down, g_final)


def _pad_ff(a, axis):
    gate, val = jnp.split(a, 2, axis=axis)
    pad = [(0, 0)] * a.ndim
    pad[axis] = (0, D_FF_PAD - D_FF)
    return jnp.concatenate([jnp.pad(gate, pad), jnp.pad(val, pad)], axis=axis)


def kernel(x, norm_mix, w_in, gm_ln_g, gm_ln_b, gm_w_s, gm_b_s, na_rpb, w_branch, w_out,
           norm_ffn, w_up, conv_k, conv_b, w_down, norm_final):
    b, t, d = x.shape
    assert (b, t, d) == (1, SEQ, D_MODEL)
    xs = x.reshape(t, d)
    bias = _na_bias(na_rpb.reshape(-1))
    g_final = norm_final.reshape(1, d)
    uv0, qkv0, gt0 = 0, 2 * GM_WIDTH, 2 * GM_WIDTH + 3 * NA_WIDTH
    for l in range(DEPTH):
        w_in_b = w_in[l].astype(_BF16)
        g_mix = norm_mix[l].reshape(1, d)
        uv = _norm_proj(xs, g_mix, w_in_b, uv0, 2 * GM_WIDTH, "gelu", "proj_uv")
        qkv = _norm_proj(xs, g_mix, w_in_b, qkv0, 3 * NA_WIDTH, "none", "proj_qkv")
        gates = _norm_proj(xs, g_mix, w_in_b, gt0, 2 * D_MODEL, "sigmoid", "proj_gates")
        ya = _gmlp(uv, gm_ln_g[l].reshape(1, -1), gm_ln_b[l].reshape(1, -1),
                   gm_w_s[l].astype(_BF16), gm_b_s[l].reshape(GM_GROUPS, CHUNK, 1))
        yb = _na(qkv, bias, l)
        xs = _merge(xs, ya, yb, gates, w_branch[l].astype(_BF16), w_out[l].astype(_BF16))
        xs = _ffn(xs, norm_ffn[l].reshape(1, d),
                  _pad_ff(w_up[l], 1).astype(_BF16), _pad_ff(conv_k[l], 1),
                  _pad_ff(conv_b[l].reshape(1, -1), 1),
                  jnp.pad(w_down[l], ((0, D_FF_PAD - D_FF), (0, 0))).astype(_BF16),
                  g_final, l == DEPTH - 1)
    return xs.reshape(b, t, d)
```

```python
from functools import partial

import jax
import jax.numpy as jnp
from jax import lax
from jax.experimental import pallas as pl
from jax.experimental.pallas import tpu as pltpu

D_MODEL = 2048
SEQ = 8192
DEPTH = 2
GRID_W = 64
ROWS = SEQ // GRID_W
GM_GROUPS = 8
GM_HEAD = 128
GM_WIDTH = GM_GROUPS * GM_HEAD
CHUNK = 128
NA_HEADS = 16
NA_HEAD_DIM = 64
NA_WIDTH = NA_HEADS * NA_HEAD_DIM
NA_KR = 8
NA_KC = 16
D_FF = 5504
RMS_EPS = 1e-6
LN_EPS = 1e-5
NEG_INF = -1e30

LANES = 128
BF16_SUBLANES = 16
VMEM_LIMIT_BYTES = 56 * 1024 * 1024

FF_TILE = 512
D_FF_PAD = ((D_FF + FF_TILE - 1) // FF_TILE) * FF_TILE
N_FF_TILES = D_FF_PAD // FF_TILE
PROJ_TM = 1024
PROJ_TN = 1024
GMLP_TM = 512
MERGE_TM = 256
FFN_TM = 512
HALO = BF16_SUBLANES
CAST_ROWS = 256
NA_KTOK = NA_KR * GRID_W
NA_ROWS_PER_STEP = 16
NA_UNROLL = 8
NA_CLASS_ROW = (0, 1, 2, 3, ROWS // 2, ROWS - 3, ROWS - 2, ROWS - 1)
N_CLASSES = len(NA_CLASS_ROW)
NA_INTERIOR_CLASS = NA_KR // 2

_BF16 = jnp.bfloat16
_F32 = jnp.float32


def _params(sem):
    return pltpu.CompilerParams(dimension_semantics=sem, vmem_limit_bytes=VMEM_LIMIT_BYTES)


def _rms(x, g):
    return x * lax.rsqrt(jnp.mean(x * x, axis=-1, keepdims=True) + RMS_EPS) * g


def _gelu_tanh(x):
    return 0.5 * x * (1.0 + jnp.tanh(0.7978845608028654 * (x + 0.044715 * (x * x * x))))


def _sigmoid(x):
    return 1.0 / (1.0 + jnp.exp(-x))


_ACTS = {"gelu": _gelu_tanh, "sigmoid": _sigmoid, "none": lambda x: x}


def _cast_kernel(x_ref, o_ref):
    o_ref[...] = x_ref[...].astype(o_ref.dtype)


def _cast_bf16(w):
    l, r, c = w.shape
    spec = pl.BlockSpec((None, CAST_ROWS, c), lambda a, b: (a, b, 0))
    return pl.pallas_call(
        _cast_kernel, grid=(l, r // CAST_ROWS), in_specs=[spec], out_specs=spec,
        out_shape=jax.ShapeDtypeStruct(w.shape, _BF16),
        compiler_params=_params(("arbitrary", "arbitrary")), name="cast_bf16",
    )(w)


def _cast_up_kernel(x_ref, o_ref):
    zeros = jnp.zeros((x_ref.shape[0], D_FF_PAD - D_FF), o_ref.dtype)
    o_ref[:, :D_FF] = x_ref[:, :D_FF].astype(o_ref.dtype)
    o_ref[:, D_FF:D_FF_PAD] = zeros
    o_ref[:, D_FF_PAD:D_FF_PAD + D_FF] = x_ref[:, D_FF:].astype(o_ref.dtype)
    o_ref[:, D_FF_PAD + D_FF:] = zeros


def _cast_up(w_up):
    l, r, c = w_up.shape
    return pl.pallas_call(
        _cast_up_kernel, grid=(l, r // CAST_ROWS),
        in_specs=[pl.BlockSpec((None, CAST_ROWS, c), lambda a, b: (a, b, 0))],
        out_specs=pl.BlockSpec((None, CAST_ROWS, 2 * D_FF_PAD), lambda a, b: (a, b, 0)),
        out_shape=jax.ShapeDtypeStruct((l, r, 2 * D_FF_PAD), _BF16),
        compiler_params=_params(("arbitrary", "arbitrary")), name="cast_w_up",
    )(w_up)


def _cast_down_kernel(x_ref, o_ref):
    row = pl.program_id(1) * FF_TILE + lax.broadcasted_iota(jnp.int32, x_ref.shape, 0)
    o_ref[...] = jnp.where(row < D_FF, x_ref[...], 0.0).astype(o_ref.dtype)


def _cast_down(w_down):
    l, _, c = w_down.shape
    spec = pl.BlockSpec((None, FF_TILE, c), lambda a, b: (a, b, 0))
    return pl.pallas_call(
        _cast_down_kernel, grid=(l, N_FF_TILES), in_specs=[spec], out_specs=spec,
        out_shape=jax.ShapeDtypeStruct((l, D_FF_PAD, c), _BF16),
        compiler_params=_params(("arbitrary", "arbitrary")), name="cast_w_down",
    )(w_down)


def _rms_kernel(x_ref, g_ref, o_ref):
    o_ref[...] = _rms(x_ref[...], g_ref[...]).astype(o_ref.dtype)


def _rms_bf16(x, g):
    m = x.shape[0]
    return pl.pallas_call(
        _rms_kernel, grid=(m // PROJ_TM,),
        in_specs=[pl.BlockSpec((PROJ_TM, D_MODEL), lambda i: (i, 0)),
                  pl.BlockSpec((1, D_MODEL), lambda i: (0, 0))],
        out_specs=pl.BlockSpec((PROJ_TM, D_MODEL), lambda i: (i, 0)),
        out_shape=jax.ShapeDtypeStruct((m, D_MODEL), _BF16),
        compiler_params=_params(("arbitrary",)), name="rms_norm",
    )(x, g)


def _proj_kernel(h_ref, w_ref, o_ref, w_scr, *, act):
    @pl.when(pl.program_id(1) == 0)
    def _():
        w_scr[...] = w_ref[...].astype(_BF16)

    acc = jnp.dot(h_ref[...], w_scr[...], preferred_element_type=_F32)
    o_ref[...] = _ACTS[act](acc).astype(o_ref.dtype)


def _proj(h, w_in, layer, col0, ncols, act, name):
    m = h.shape[0]
    jb0 = col0 // PROJ_TN
    return pl.pallas_call(
        partial(_proj_kernel, act=act),
        grid=(ncols // PROJ_TN, m // PROJ_TM),
        in_specs=[
            pl.BlockSpec((PROJ_TM, D_MODEL), lambda j, i: (i, 0)),
            pl.BlockSpec((None, D_MODEL, PROJ_TN), lambda j, i: (layer, 0, jb0 + j)),
        ],
        out_specs=pl.BlockSpec((PROJ_TM, PROJ_TN), lambda j, i: (i, j)),
        out_shape=jax.ShapeDtypeStruct((m, ncols), _BF16),
        scratch_shapes=[pltpu.VMEM((D_MODEL, PROJ_TN), _BF16)],
        compiler_params=_params(("arbitrary", "arbitrary")),
        name=name,
    )(h, w_in)


def _gmlp_kernel(uv_ref, lng_ref, lnb_ref, ws_ref, bs_ref, o_ref):
    v = uv_ref[:, GM_WIDTH:].astype(_F32)
    mu = jnp.mean(v, axis=-1, keepdims=True)
    vc = v - mu
    var = jnp.mean(vc * vc, axis=-1, keepdims=True)
    vn = (vc * lax.rsqrt(var + LN_EPS) * lng_ref[...] + lnb_ref[...]).astype(_BF16)
    for n in range(GMLP_TM // CHUNK):
        rows = slice(n * CHUNK, (n + 1) * CHUNK)
        for g in range(GM_GROUPS):
            cols = slice(g * GM_HEAD, (g + 1) * GM_HEAD)
            mixed = jnp.dot(ws_ref[g], vn[rows, cols], preferred_element_type=_F32) + bs_ref[g]
            o_ref[rows, cols] = (uv_ref[rows, cols].astype(_F32) * mixed).astype(o_ref.dtype)


def _gmlp(uv, ln_g, ln_b, w_s, b_s):
    m = uv.shape[0]
    return pl.pallas_call(
        _gmlp_kernel,
        grid=(m // GMLP_TM,),
        in_specs=[
            pl.BlockSpec((GMLP_TM, 2 * GM_WIDTH), lambda i: (i, 0)),
            pl.BlockSpec((1, GM_WIDTH), lambda i: (0, 0)),
            pl.BlockSpec((1, GM_WIDTH), lambda i: (0, 0)),
            pl.BlockSpec((GM_GROUPS, CHUNK, CHUNK), lambda i: (0, 0, 0)),
            pl.BlockSpec((GM_GROUPS, CHUNK, 1), lambda i: (0, 0, 0)),
        ],
        out_specs=pl.BlockSpec((GMLP_TM, GM_WIDTH), lambda i: (i, 0)),
        out_shape=jax.ShapeDtypeStruct((m, GM_WIDTH), _BF16),
        compiler_params=_params(("arbitrary",)),
        name="gmlp_gate",
    )(uv, ln_g, ln_b, w_s, b_s)


def _na_bias_kernel(rpb_ref, o_ref):
    lh = pl.program_id(0)
    n_dr, n_dc = 2 * NA_KR - 1, 2 * NA_KC - 1
    c = lax.broadcasted_iota(jnp.int32, (GRID_W, GRID_W), 0)
    k = lax.broadcasted_iota(jnp.int32, (GRID_W, GRID_W), 1)
    cs = jnp.clip(c - NA_KC // 2, 0, GRID_W - NA_KC)
    in_win = (k >= cs) & (k < cs + NA_KC)
    dc = jnp.clip(k - c, -(NA_KC - 1), NA_KC - 1) + (NA_KC - 1)
    dc_eq = [dc == d for d in range(n_dc)]
    neg = jnp.full((GRID_W, GRID_W), NEG_INF, _F32)
    tiles = []
    for dr in range(n_dr):
        t = neg
        for d in range(n_dc):
            t = jnp.where(dc_eq[d], rpb_ref[(lh * n_dr + dr) * n_dc + d], t)
        tiles.append(jnp.where(in_win, t, neg))
    for cls, r in enumerate(NA_CLASS_ROW):
        rs = min(max(r - NA_KR // 2, 0), ROWS - NA_KR)
        for i in range(NA_KR):
            o_ref[0, cls, :, i * GRID_W:(i + 1) * GRID_W] = tiles[rs + i - r + NA_KR - 1]


def _na_bias(rpb_flat):
    n = DEPTH * NA_HEADS
    return pl.pallas_call(
        _na_bias_kernel,
        grid=(n,),
        in_specs=[pl.BlockSpec(memory_space=pltpu.SMEM)],
        out_specs=pl.BlockSpec((1, N_CLASSES, GRID_W, NA_KTOK), lambda i: (i // 2, 0, i % 2, 0)),
        out_shape=jax.ShapeDtypeStruct((n // 2, N_CLASSES, 2 * GRID_W, NA_KTOK), _F32),
        compiler_params=_params(("arbitrary",)),
        name="na_bias",
    )(rpb_flat)


def _na_window(r):
    rs = jnp.clip(r - NA_KR // 2, 0, ROWS - NA_KR)
    cls = jnp.where(r < NA_INTERIOR_CLASS, r,
                    jnp.where(r > ROWS - NA_KR // 2, r - (ROWS - NA_KR), NA_INTERIOR_CLASS))
    return pl.ds(pl.multiple_of(rs * GRID_W, GRID_W), NA_KTOK), cls


def _na_kernel(q_ref, k_ref, v_ref, b_ref, o_ref, p_scr, inv_scr):
    rb = pl.program_id(1)
    n_rb = pl.num_programs(1) - 1
    scale = NA_HEAD_DIM ** -0.5
    head0 = lax.broadcasted_iota(jnp.int32, (GRID_W, LANES), 1) < NA_HEAD_DIM

    @pl.when(rb == 0)
    def _():
        p_scr[...] = jnp.zeros_like(p_scr)
        inv_scr[...] = jnp.zeros_like(inv_scr)

    row_s = jnp.minimum(rb, n_rb - 1) * NA_ROWS_PER_STEP
    row_o = jnp.maximum(rb - 1, 0) * NA_ROWS_PER_STEP

    def group(g, carry):
        for u in range(NA_UNROLL):
            t = g * NA_UNROLL + u
            qrows = pl.ds(pl.multiple_of(t * GRID_W, GRID_W), GRID_W)
            krows, cls = _na_window(row_s + t)
            q = q_ref[qrows, :] * scale
            zero = jnp.zeros_like(q)
            q2 = jnp.concatenate([jnp.where(head0, q, zero), jnp.where(head0, zero, q)], axis=0)
            s = lax.dot_general(q2, k_ref[krows, :], (((1,), (1,)), ((), ())),
                                preferred_element_type=_F32) + b_ref[0, cls]
            p = jnp.exp(s - jnp.max(s, axis=-1, keepdims=True))
            den = jnp.sum(p, axis=-1, keepdims=True)

            vrows, _ = _na_window(row_o + t)
            o = jnp.dot(p_scr[t], v_ref[vrows, :], preferred_element_type=_F32) * inv_scr[t]
            o_ref[qrows, :] = jnp.where(head0, o[:GRID_W], o[GRID_W:]).astype(o_ref.dtype)

            p_scr[t] = p.astype(_BF16)
            inv_scr[t] = jnp.broadcast_to(1.0 / den, (2 * GRID_W, LANES))
        return carry

    lax.fori_loop(0, NA_ROWS_PER_STEP // NA_UNROLL, group, 0)


def _na(qkv, bias, layer):
    m = qkv.shape[0]
    tq = NA_ROWS_PER_STEP * GRID_W
    n_hp = NA_WIDTH // LANES
    n_rb = m // tq
    return pl.pallas_call(
        _na_kernel,
        grid=(n_hp, n_rb + 1),
        in_specs=[
            pl.BlockSpec((tq, LANES), lambda hp, rb: (jnp.minimum(rb, n_rb - 1), hp)),
            pl.BlockSpec((m, LANES), lambda hp, rb: (0, n_hp + hp)),
            pl.BlockSpec((m, LANES), lambda hp, rb: (0, 2 * n_hp + hp)),
            pl.BlockSpec((1, N_CLASSES, 2 * GRID_W, NA_KTOK),
                         lambda hp, rb: (layer * n_hp + hp, 0, 0, 0)),
        ],
        out_specs=pl.BlockSpec((tq, LANES), lambda hp, rb: (jnp.maximum(rb - 1, 0), hp)),
        out_shape=jax.ShapeDtypeStruct((m, NA_WIDTH), _BF16),
        scratch_shapes=[
            pltpu.VMEM((NA_ROWS_PER_STEP, 2 * GRID_W, NA_KTOK), _BF16),
            pltpu.VMEM((NA_ROWS_PER_STEP, 2 * GRID_W, LANES), _F32),
        ],
        compiler_params=_params(("arbitrary", "arbitrary")),
        name="na_attn",
    )(qkv, qkv, qkv, bias)


def _merge_kernel(x_ref, ya_ref, yb_ref, gt_ref, wb_ref, wo_ref, g_ref, o_ref, h_ref):
    ta = jnp.dot(ya_ref[...], wb_ref[:GM_WIDTH, :], preferred_element_type=_F32)
    tb = jnp.dot(yb_ref[...], wb_ref[GM_WIDTH:, :], preferred_element_type=_F32)
    merged = (gt_ref[:, :D_MODEL].astype(_F32) * ta + gt_ref[:, D_MODEL:].astype(_F32) * tb)
    y = x_ref[...] + jnp.dot(merged.astype(_BF16), wo_ref[...], preferred_element_type=_F32)
    o_ref[...] = y
    h_ref[...] = _rms(y, g_ref[...]).astype(h_ref.dtype)


def _merge(x, ya, yb, gates, w_branch, w_out, layer, g_ffn):
    m = x.shape[0]
    resident = dict(pipeline_mode=pl.Buffered(1))
    row = lambda i: (i, 0)
    return pl.pallas_call(
        _merge_kernel,
        grid=(m // MERGE_TM,),
        in_specs=[
            pl.BlockSpec((MERGE_TM, D_MODEL), row),
            pl.BlockSpec((MERGE_TM, GM_WIDTH), row),
            pl.BlockSpec((MERGE_TM, NA_WIDTH), row),
            pl.BlockSpec((MERGE_TM, 2 * D_MODEL), row),
            pl.BlockSpec((None, GM_WIDTH + NA_WIDTH, D_MODEL), lambda i: (layer, 0, 0), **resident),
            pl.BlockSpec((None, D_MODEL, D_MODEL), lambda i: (layer, 0, 0), **resident),
            pl.BlockSpec((1, D_MODEL), lambda i: (0, 0)),
        ],
        out_specs=[pl.BlockSpec((MERGE_TM, D_MODEL), row), pl.BlockSpec((MERGE_TM, D_MODEL), row)],
        out_shape=[jax.ShapeDtypeStruct((m, D_MODEL), _F32),
                   jax.ShapeDtypeStruct((m, D_MODEL), _BF16)],
        compiler_params=_params(("arbitrary",)),
        name="merge_out",
    )(x, ya, yb, gates, w_branch, w_out, g_ffn)


def _ffn_kernel(x_ref, h_ref, hp_ref, hn_ref, wg_ref, wv_ref, kg_ref, kv_ref, bg_ref, bv_ref,
                wd_ref, gn_ref, *refs, final_norm):
    if final_norm:
        o_ref, h_scr, ug_scr, uv_scr = refs
    else:
        o_ref, ho_ref, h_scr, ug_scr, uv_scr = refs
    i, j = pl.program_id(0), pl.program_id(1)
    tm = FFN_TM

    @pl.when(j == 0)
    def _():
        zero = jnp.zeros((HALO, D_MODEL), _BF16)
        h_scr[:HALO, :] = jnp.where(i > 0, hp_ref[...], zero)
        h_scr[HALO:HALO + tm, :] = h_ref[...]
        h_scr[HALO + tm:, :] = jnp.where(i < pl.num_programs(0) - 1, hn_ref[...], zero)
        o_ref[...] = x_ref[...]

    h = h_scr[...]
    ug_scr[...] = jnp.dot(h, wg_ref[...], preferred_element_type=_F32)
    uv_scr[...] = jnp.dot(h, wv_ref[...], preferred_element_type=_F32)

    def conv(u_scr, k_ref, b_ref):
        return (u_scr[HALO - 1:HALO - 1 + tm, :] * k_ref[0:1, :]
                + u_scr[HALO:HALO + tm, :] * k_ref[1:2, :]
                + u_scr[HALO + 1:HALO + 1 + tm, :] * k_ref[2:3, :] + b_ref[...])

    gate = conv(ug_scr, kg_ref, bg_ref)
    val = conv(uv_scr, kv_ref, bv_ref)
    act = (gate * _sigmoid(gate) * val).astype(_BF16)
    o_ref[...] += jnp.dot(act, wd_ref[...], preferred_element_type=_F32)

    @pl.when(j == pl.num_programs(1) - 1)
    def _():
        hn = _rms(o_ref[...], gn_ref[...])
        if final_norm:
            o_ref[...] = hn
        else:
            ho_ref[...] = hn.astype(ho_ref.dtype)


def _ffn(x, h, w_up, conv_k, conv_b, w_down, layer, g_next, final_norm):
    m = x.shape[0]
    tm = FFN_TM
    nb = tm // HALO
    last = m // HALO - 1
    row = lambda i, j: (i, 0)
    gate = lambda i, j: (layer, 0, j)
    val = lambda i, j: (layer, 0, N_FF_TILES + j)
    out_specs = [pl.BlockSpec((tm, D_MODEL), row)]
    out_shape = [jax.ShapeDtypeStruct((m, D_MODEL), _F32)]
    if not final_norm:
        out_specs.append(pl.BlockSpec((tm, D_MODEL), row))
        out_shape.append(jax.ShapeDtypeStruct((m, D_MODEL), _BF16))
    return pl.pallas_call(
        partial(_ffn_kernel, final_norm=final_norm),
        grid=(m // tm, N_FF_TILES),
        in_specs=[
            pl.BlockSpec((tm, D_MODEL), row),
            pl.BlockSpec((tm, D_MODEL), row),
            pl.BlockSpec((HALO, D_MODEL), lambda i, j: (jnp.maximum(i * nb - 1, 0), 0)),
            pl.BlockSpec((HALO, D_MODEL), lambda i, j: (jnp.minimum((i + 1) * nb, last), 0)),
            pl.BlockSpec((None, D_MODEL, FF_TILE), gate),
            pl.BlockSpec((None, D_MODEL, FF_TILE), val),
            pl.BlockSpec((None, 3, FF_TILE), gate),
            pl.BlockSpec((None, 3, FF_TILE), val),
            pl.BlockSpec((None, 1, FF_TILE), gate),
            pl.BlockSpec((None, 1, FF_TILE), val),
            pl.BlockSpec((None, FF_TILE, D_MODEL), lambda i, j: (layer, j, 0)),
            pl.BlockSpec((1, D_MODEL), lambda i, j: (0, 0)),
        ],
        out_specs=out_specs,
        out_shape=out_shape,
        scratch_shapes=[
            pltpu.VMEM((tm + 2 * HALO, D_MODEL), _BF16),
            pltpu.VMEM((tm + 2 * HALO, FF_TILE), _F32),
            pltpu.VMEM((tm + 2 * HALO, FF_TILE), _F32),
        ],
        compiler_params=_params(("arbitrary", "arbitrary")),
        name="conv_ffn",
    )(x, h, h, h, w_up, w_up, conv_k, conv_k, conv_b, conv_b, w_down, g_next)


def _pad_ff(a):
    gate, val = jnp.split(a, 2, axis=-1)
    pad = [(0, 0)] * (a.ndim - 1) + [(0, D_FF_PAD - D_FF)]
    return jnp.concatenate([jnp.pad(gate, pad), jnp.pad(val, pad)], axis=-1)


def kernel(x, norm_mix, w_in, gm_ln_g, gm_ln_b, gm_w_s, gm_b_s, na_rpb, w_branch, w_out,
           norm_ffn, w_up, conv_k, conv_b, w_down, norm_final):
    b, t, d = x.shape
    assert (b, t, d) == (1, SEQ, D_MODEL)
    xs = x.reshape(t, d)
    bias = _na_bias(na_rpb.reshape(-1))
    w_branch_b, w_out_b = _cast_bf16(w_branch), _cast_bf16(w_out)
    w_up_b, w_down_b = _cast_up(w_up), _cast_down(w_down)
    conv_k_p, conv_b_p = _pad_ff(conv_k), _pad_ff(conv_b.reshape(DEPTH, 1, -1))
    w_s_b = gm_w_s.astype(_BF16)
    uv0, qkv0, gt0 = 0, 2 * GM_WIDTH, 2 * GM_WIDTH + 3 * NA_WIDTH
    h = _rms_bf16(xs, norm_mix[0].reshape(1, d))
    for l in range(DEPTH):
        uv = _proj(h, w_in, l, uv0, 2 * GM_WIDTH, "gelu", "proj_uv")
        qkv = _proj(h, w_in, l, qkv0, 3 * NA_WIDTH, "none", "proj_qkv")
        gates = _proj(h, w_in, l, gt0, 2 * D_MODEL, "sigmoid", "proj_gates")
        ya = _gmlp(uv, gm_ln_g[l].reshape(1, -1), gm_ln_b[l].reshape(1, -1),
                   w_s_b[l], gm_b_s[l].reshape(GM_GROUPS, CHUNK, 1))
        yb = _na(qkv, bias, l)
        xs, h = _merge(xs, ya, yb, gates, w_branch_b, w_out_b, l, norm_ffn[l].reshape(1, d))
        if l == DEPTH - 1:
            xs, = _ffn(xs, h, w_up_b, conv_k_p, conv_b_p, w_down_b, l,
                       norm_final.reshape(1, d), True)
        else:
            xs, h = _ffn(xs, h, w_up_b, conv_k_p, conv_b_p, w_down_b, l,
                         norm_mix[l + 1].reshape(1, d), False)
    return xs.reshape(b, t, d)
```

```python
from functools import partial
from typing import Callable, NamedTuple

import jax
import jax.numpy as jnp
from jax import lax
from jax.experimental import pallas as pl
from jax.experimental.pallas import tpu as pltpu

D_MODEL = 2048
SEQ = 8192
DEPTH = 2
GRID_W = 64
ROWS = SEQ // GRID_W
GM_GROUPS = 8
GM_HEAD = 128
GM_WIDTH = GM_GROUPS * GM_HEAD
CHUNK = 128
NA_HEADS = 16
NA_HEAD_DIM = 64
NA_WIDTH = NA_HEADS * NA_HEAD_DIM
NA_KR = 8
NA_KC = 16
D_FF = 5504
RMS_EPS = 1e-6
LN_EPS = 1e-5
NEG_INF = -1e30

LANES = 128
BF16_SUBLANES = 16
VMEM_BYTES = 64 * 1024 * 1024
VMEM_LIMIT_BYTES = VMEM_BYTES - 4 * 1024 * 1024

FF_TILE = 512
D_FF_PAD = ((D_FF + FF_TILE - 1) // FF_TILE) * FF_TILE
N_FF_TILES = D_FF_PAD // FF_TILE
PROJ_TM = 1024
PROJ_TN = 1024
GMLP_TM = 512
MERGE_TM = 256
FFN_TM = 1024
HALO = BF16_SUBLANES
NA_KTOK = NA_KR * GRID_W
NA_ROWS_PER_STEP = 16
NA_UNROLL = 8
NA_CLASS_ROW = (0, 1, 2, 3, ROWS // 2, ROWS - 3, ROWS - 2, ROWS - 1)
N_CLASSES = len(NA_CLASS_ROW)
NA_INTERIOR_CLASS = NA_KR // 2

_BF16 = jnp.bfloat16
_F32 = jnp.float32


def _params(sem):
    return pltpu.CompilerParams(dimension_semantics=sem, vmem_limit_bytes=VMEM_LIMIT_BYTES)


def _rms(x, g):
    return x * lax.rsqrt(jnp.mean(x * x, axis=-1, keepdims=True) + RMS_EPS) * g


def _gelu_tanh(x):
    return 0.5 * x * (1.0 + jnp.tanh(0.7978845608028654 * (x + 0.044715 * (x * x * x))))


def _sigmoid(x):
    return 1.0 / (1.0 + jnp.exp(-x))


_ACTS = {"gelu": _gelu_tanh, "sigmoid": _sigmoid, "none": lambda x: x}


class SideCast(NamedTuple):
    body: Callable
    array: jax.Array
    rows: int
    out_rows: int
    out_cols: int


def _cast_rows(x_ref, o_ref, step):
    del step
    o_ref[...] = x_ref[...].astype(o_ref.dtype)


def _cast_up_rows(x_ref, o_ref, step):
    del step
    zeros = jnp.zeros((x_ref.shape[0], D_FF_PAD - D_FF), o_ref.dtype)
    o_ref[:, :D_FF] = x_ref[:, :D_FF].astype(o_ref.dtype)
    o_ref[:, D_FF:D_FF_PAD] = zeros
    o_ref[:, D_FF_PAD:D_FF_PAD + D_FF] = x_ref[:, D_FF:].astype(o_ref.dtype)
    o_ref[:, D_FF_PAD + D_FF:] = zeros


def _cast_down_rows(x_ref, o_ref, step):
    row = step * x_ref.shape[0] + lax.broadcasted_iota(jnp.int32, x_ref.shape, 0)
    o_ref[...] = jnp.where(row < D_FF, x_ref[...], 0.0).astype(o_ref.dtype)


def _rms_kernel(x_ref, g_ref, o_ref):
    o_ref[...] = _rms(x_ref[...], g_ref[...]).astype(o_ref.dtype)


def _rms_bf16(x, g):
    m = x.shape[0]
    return pl.pallas_call(
        _rms_kernel, grid=(m // PROJ_TM,),
        in_specs=[pl.BlockSpec((PROJ_TM, D_MODEL), lambda i: (i, 0)),
                  pl.BlockSpec((1, D_MODEL), lambda i: (0, 0))],
        out_specs=pl.BlockSpec((PROJ_TM, D_MODEL), lambda i: (i, 0)),
        out_shape=jax.ShapeDtypeStruct((m, D_MODEL), _BF16),
        compiler_params=_params(("arbitrary",)), name="rms_norm",
    )(x, g)


def _proj_kernel(*refs, act, side_bodies):
    n = len(side_bodies)
    h_ref, w_ref = refs[:2]
    side_in = refs[2:2 + n]
    o_ref = refs[2 + n]
    side_out = refs[3 + n:3 + 2 * n]
    w_scr = refs[3 + 2 * n]

    @pl.when(pl.program_id(1) == 0)
    def _():
        w_scr[...] = w_ref[...].astype(_BF16)

    step = pl.program_id(0) * pl.num_programs(1) + pl.program_id(1)
    for body, x_ref, y_ref in zip(side_bodies, side_in, side_out):
        body(x_ref, y_ref, step)

    acc = jnp.dot(h_ref[...], w_scr[...], preferred_element_type=_F32)
    o_ref[...] = _ACTS[act](acc).astype(o_ref.dtype)


def _proj(h, w_in, layer, col0, ncols, act, name, sides=()):
    m = h.shape[0]
    jb0 = col0 // PROJ_TN
    n_i = m // PROJ_TM
    grid = (ncols // PROJ_TN, n_i)
    slab = lambda j, i: (j * n_i + i, 0)
    for s in sides:
        assert s.rows * grid[0] * grid[1] == s.out_rows, (name, s.rows, s.out_rows)
    outs = pl.pallas_call(
        partial(_proj_kernel, act=act, side_bodies=tuple(s.body for s in sides)),
        grid=grid,
        in_specs=[
            pl.BlockSpec((PROJ_TM, D_MODEL), lambda j, i: (i, 0)),
            pl.BlockSpec((None, D_MODEL, PROJ_TN), lambda j, i: (layer, 0, jb0 + j)),
        ] + [pl.BlockSpec((None, s.rows, s.array.shape[2]), lambda j, i: (layer, j * n_i + i, 0))
             for s in sides],
        out_specs=[pl.BlockSpec((PROJ_TM, PROJ_TN), lambda j, i: (i, j))]
        + [pl.BlockSpec((s.rows, s.out_cols), slab) for s in sides],
        out_shape=[jax.ShapeDtypeStruct((m, ncols), _BF16)]
        + [jax.ShapeDtypeStruct((s.out_rows, s.out_cols), _BF16) for s in sides],
        scratch_shapes=[pltpu.VMEM((D_MODEL, PROJ_TN), _BF16)],
        compiler_params=_params(("arbitrary", "arbitrary")),
        name=name,
    )(h, w_in, *[s.array for s in sides])
    return outs


def _gmlp_kernel(uv_ref, lng_ref, lnb_ref, ws_ref, bs_ref, o_ref):
    v = uv_ref[:, GM_WIDTH:].astype(_F32)
    mu = jnp.mean(v, axis=-1, keepdims=True)
    vc = v - mu
    var = jnp.mean(vc * vc, axis=-1, keepdims=True)
    vn = (vc * lax.rsqrt(var + LN_EPS) * lng_ref[...] + lnb_ref[...]).astype(_BF16)
    for n in range(GMLP_TM // CHUNK):
        rows = slice(n * CHUNK, (n + 1) * CHUNK)
        for g in range(GM_GROUPS):
            cols = slice(g * GM_HEAD, (g + 1) * GM_HEAD)
            mixed = jnp.dot(ws_ref[g], vn[rows, cols], preferred_element_type=_F32) + bs_ref[g]
            o_ref[rows, cols] = (uv_ref[rows, cols].astype(_F32) * mixed).astype(o_ref.dtype)


def _gmlp(uv, ln_g, ln_b, w_s, b_s):
    m = uv.shape[0]
    return pl.pallas_call(
        _gmlp_kernel,
        grid=(m // GMLP_TM,),
        in_specs=[
            pl.BlockSpec((GMLP_TM, 2 * GM_WIDTH), lambda i: (i, 0)),
            pl.BlockSpec((1, GM_WIDTH), lambda i: (0, 0)),
            pl.BlockSpec((1, GM_WIDTH), lambda i: (0, 0)),
            pl.BlockSpec((GM_GROUPS, CHUNK, CHUNK), lambda i: (0, 0, 0)),
            pl.BlockSpec((GM_GROUPS, CHUNK, 1), lambda i: (0, 0, 0)),
        ],
        out_specs=pl.BlockSpec((GMLP_TM, GM_WIDTH), lambda i: (i, 0)),
        out_shape=jax.ShapeDtypeStruct((m, GM_WIDTH), _BF16),
        compiler_params=_params(("arbitrary",)),
        name="gmlp_gate",
    )(uv, ln_g, ln_b, w_s, b_s)


def _na_bias_kernel(rpb_ref, o_ref):
    lh = pl.program_id(0)
    n_dr, n_dc = 2 * NA_KR - 1, 2 * NA_KC - 1
    c = lax.broadcasted_iota(jnp.int32, (GRID_W, GRID_W), 0)
    k = lax.broadcasted_iota(jnp.int32, (GRID_W, GRID_W), 1)
    cs = jnp.clip(c - NA_KC // 2, 0, GRID_W - NA_KC)
    in_win = (k >= cs) & (k < cs + NA_KC)
    dc = jnp.clip(k - c, -(NA_KC - 1), NA_KC - 1) + (NA_KC - 1)
    dc_eq = [dc == d for d in range(n_dc)]
    neg = jnp.full((GRID_W, GRID_W), NEG_INF, _F32)
    tiles = []
    for dr in range(n_dr):
        t = neg
        for d in range(n_dc):
            t = jnp.where(dc_eq[d], rpb_ref[(lh * n_dr + dr) * n_dc + d], t)
        tiles.append(jnp.where(in_win, t, neg))
    for cls, r in enumerate(NA_CLASS_ROW):
        rs = min(max(r - NA_KR // 2, 0), ROWS - NA_KR)
        for i in range(NA_KR):
            o_ref[0, cls, :, i * GRID_W:(i + 1) * GRID_W] = tiles[rs + i - r + NA_KR - 1]


def _na_bias(rpb_flat):
    n = DEPTH * NA_HEADS
    return pl.pallas_call(
        _na_bias_kernel,
        grid=(n,),
        in_specs=[pl.BlockSpec(memory_space=pltpu.SMEM)],
        out_specs=pl.BlockSpec((1, N_CLASSES, GRID_W, NA_KTOK), lambda i: (i // 2, 0, i % 2, 0)),
        out_shape=jax.ShapeDtypeStruct((n // 2, N_CLASSES, 2 * GRID_W, NA_KTOK), _F32),
        compiler_params=_params(("arbitrary",)),
        name="na_bias",
    )(rpb_flat)


def _na_window(r):
    rs = jnp.clip(r - NA_KR // 2, 0, ROWS - NA_KR)
    cls = jnp.where(r < NA_INTERIOR_CLASS, r,
                    jnp.where(r > ROWS - NA_KR // 2, r - (ROWS - NA_KR), NA_INTERIOR_CLASS))
    return pl.ds(pl.multiple_of(rs * GRID_W, GRID_W), NA_KTOK), cls


def _na_kernel(q_ref, k_ref, v_ref, b_ref, o_ref, p_scr, inv_scr):
    rb = pl.program_id(1)
    n_rb = pl.num_programs(1) - 1
    scale = NA_HEAD_DIM ** -0.5
    head0 = lax.broadcasted_iota(jnp.int32, (GRID_W, LANES), 1) < NA_HEAD_DIM

    @pl.when(rb == 0)
    def _():
        p_scr[...] = jnp.zeros_like(p_scr)
        inv_scr[...] = jnp.zeros_like(inv_scr)

    row_s = jnp.minimum(rb, n_rb - 1) * NA_ROWS_PER_STEP
    row_o = jnp.maximum(rb - 1, 0) * NA_ROWS_PER_STEP

    def group(g, carry):
        for u in range(NA_UNROLL):
            t = g * NA_UNROLL + u
            qrows = pl.ds(pl.multiple_of(t * GRID_W, GRID_W), GRID_W)
            krows, cls = _na_window(row_s + t)
            q = q_ref[qrows, :] * scale
            zero = jnp.zeros_like(q)
            q2 = jnp.concatenate([jnp.where(head0, q, zero), jnp.where(head0, zero, q)], axis=0)
            s = lax.dot_general(q2, k_ref[krows, :], (((1,), (1,)), ((), ())),
                                preferred_element_type=_F32) + b_ref[0, cls]
            p = jnp.exp(s - jnp.max(s, axis=-1, keepdims=True))
            den = jnp.sum(p, axis=-1, keepdims=True)

            vrows, _ = _na_window(row_o + t)
            o = jnp.dot(p_scr[t], v_ref[vrows, :], preferred_element_type=_F32) * inv_scr[t]
            o_ref[qrows, :] = jnp.where(head0, o[:GRID_W], o[GRID_W:]).astype(o_ref.dtype)

            p_scr[t] = p.astype(_BF16)
            inv_scr[t] = jnp.broadcast_to(1.0 / den, (2 * GRID_W, LANES))
        return carry

    lax.fori_loop(0, NA_ROWS_PER_STEP // NA_UNROLL, group, 0)


def _na(qkv, bias, layer):
    m = qkv.shape[0]
    tq = NA_ROWS_PER_STEP * GRID_W
    n_hp = NA_WIDTH // LANES
    n_rb = m // tq
    return pl.pallas_call(
        _na_kernel,
        grid=(n_hp, n_rb + 1),
        in_specs=[
            pl.BlockSpec((tq, LANES), lambda hp, rb: (jnp.minimum(rb, n_rb - 1), hp)),
            pl.BlockSpec((m, LANES), lambda hp, rb: (0, n_hp + hp)),
            pl.BlockSpec((m, LANES), lambda hp, rb: (0, 2 * n_hp + hp)),
            pl.BlockSpec((1, N_CLASSES, 2 * GRID_W, NA_KTOK),
                         lambda hp, rb: (layer * n_hp + hp, 0, 0, 0)),
        ],
        out_specs=pl.BlockSpec((tq, LANES), lambda hp, rb: (jnp.maximum(rb - 1, 0), hp)),
        out_shape=jax.ShapeDtypeStruct((m, NA_WIDTH), _BF16),
        scratch_shapes=[
            pltpu.VMEM((NA_ROWS_PER_STEP, 2 * GRID_W, NA_KTOK), _BF16),
            pltpu.VMEM((NA_ROWS_PER_STEP, 2 * GRID_W, LANES), _F32),
        ],
        compiler_params=_params(("arbitrary", "arbitrary")),
        name="na_attn",
    )(qkv, qkv, qkv, bias)


def _merge_kernel(x_ref, ya_ref, yb_ref, gt_ref, wb_ref, wo_ref, g_ref, o_ref, h_ref):
    ta = jnp.dot(ya_ref[...], wb_ref[:GM_WIDTH, :], preferred_element_type=_F32)
    tb = jnp.dot(yb_ref[...], wb_ref[GM_WIDTH:, :], preferred_element_type=_F32)
    merged = (gt_ref[:, :D_MODEL].astype(_F32) * ta + gt_ref[:, D_MODEL:].astype(_F32) * tb)
    y = x_ref[...] + jnp.dot(merged.astype(_BF16), wo_ref[...], preferred_element_type=_F32)
    o_ref[...] = y
    h_ref[...] = _rms(y, g_ref[...]).astype(h_ref.dtype)


def _merge(x, ya, yb, gates, w_branch, w_out, g_ffn):
    m = x.shape[0]
    resident = dict(pipeline_mode=pl.Buffered(1))
    row = lambda i: (i, 0)
    return pl.pallas_call(
        _merge_kernel,
        grid=(m // MERGE_TM,),
        in_specs=[
            pl.BlockSpec((MERGE_TM, D_MODEL), row),
            pl.BlockSpec((MERGE_TM, GM_WIDTH), row),
            pl.BlockSpec((MERGE_TM, NA_WIDTH), row),
            pl.BlockSpec((MERGE_TM, 2 * D_MODEL), row),
            pl.BlockSpec((GM_WIDTH + NA_WIDTH, D_MODEL), lambda i: (0, 0), **resident),
            pl.BlockSpec((D_MODEL, D_MODEL), lambda i: (0, 0), **resident),
            pl.BlockSpec((1, D_MODEL), lambda i: (0, 0)),
        ],
        out_specs=[pl.BlockSpec((MERGE_TM, D_MODEL), row), pl.BlockSpec((MERGE_TM, D_MODEL), row)],
        out_shape=[jax.ShapeDtypeStruct((m, D_MODEL), _F32),
                   jax.ShapeDtypeStruct((m, D_MODEL), _BF16)],
        compiler_params=_params(("arbitrary",)),
        name="merge_out",
    )(x, ya, yb, gates, w_branch, w_out, g_ffn)


def _ffn_kernel(x_hbm, h_ref, hp_ref, hn_ref, wg_ref, wv_ref, kg_ref, kv_ref, bg_ref, bv_ref,
                wd_ref, gn_ref, *refs, final_norm):
    if final_norm:
        o_ref, h_scr, ug_scr, uv_scr, act_scr, x_sem = refs
    else:
        o_ref, ho_ref, h_scr, ug_scr, uv_scr, act_scr, x_sem = refs
    i, j = pl.program_id(0), pl.program_id(1)
    n_ff = pl.num_programs(1) - 1
    tm = FFN_TM

    def x_copy():
        rows = pl.ds(pl.multiple_of(i * tm, tm), tm)
        return pltpu.make_async_copy(x_hbm.at[rows, :], o_ref, x_sem)

    def up_dots():
        h = h_scr[...]
        for u_scr, w_ref in ((ug_scr, wg_ref), (uv_scr, wv_ref)):
            u = jnp.dot(h, w_ref[...], preferred_element_type=_F32)
            for c in range(FF_TILE // LANES):
                u_scr[c] = u[:, c * LANES:(c + 1) * LANES]

    def conv(u_scr, k_ref, b_ref, c):
        cols = slice(c * LANES, (c + 1) * LANES)
        return (u_scr[c, HALO - 1:HALO - 1 + tm, :] * k_ref[0:1, cols]
                + u_scr[c, HALO:HALO + tm, :] * k_ref[1:2, cols]
                + u_scr[c, HALO + 1:HALO + 1 + tm, :] * k_ref[2:3, cols] + b_ref[:, cols])

    def gate_into(slot):
        for c in range(FF_TILE // LANES):
            gate = conv(ug_scr, kg_ref, bg_ref, c)
            val = conv(uv_scr, kv_ref, bv_ref, c)
            act_scr[slot, :, c * LANES:(c + 1) * LANES] = (
                gate * _sigmoid(gate) * val).astype(_BF16)

    def down_from(slot):
        o_ref[...] += jnp.dot(act_scr[slot], wd_ref[...], preferred_element_type=_F32)

    @pl.when(j == 0)
    def _():
        zero = jnp.zeros((HALO, D_MODEL), _BF16)
        h_scr[:HALO, :] = jnp.where(i > 0, hp_ref[...], zero)
        h_scr[HALO:HALO + tm, :] = h_ref[...]
        h_scr[HALO + tm:, :] = jnp.where(i < pl.num_programs(0) - 1, hn_ref[...], zero)
        x_copy().start()
        up_dots()
        gate_into(0)

    @pl.when(j == 1)
    def _():
        x_copy().wait()

    for parity in range(2):
        @pl.when((j > 0) & (j < n_ff) & (j % 2 == parity))
        def _():
            up_dots()
            down_from(1 - parity)
            gate_into(parity)

    @pl.when(j == n_ff)
    def _():
        down_from((N_FF_TILES - 1) % 2)
        hn = _rms(o_ref[...], gn_ref[...])
        if final_norm:
            o_ref[...] = hn
        else:
            ho_ref[...] = hn.astype(ho_ref.dtype)


def _ffn(x, h, w_up, conv_k, conv_b, w_down, layer, g_next, final_norm):
    m = x.shape[0]
    tm = FFN_TM
    nb = tm // HALO
    last = m // HALO - 1
    row = lambda i, j: (i, 0)
    up_tile = lambda j: jnp.minimum(j, N_FF_TILES - 1)
    gate = lambda i, j: (layer, 0, up_tile(j))
    val = lambda i, j: (layer, 0, N_FF_TILES + up_tile(j))
    out_specs = [pl.BlockSpec((tm, D_MODEL), row)]
    out_shape = [jax.ShapeDtypeStruct((m, D_MODEL), _F32)]
    if not final_norm:
        out_specs.append(pl.BlockSpec((tm, D_MODEL), row))
        out_shape.append(jax.ShapeDtypeStruct((m, D_MODEL), _BF16))
    return pl.pallas_call(
        partial(_ffn_kernel, final_norm=final_norm),
        grid=(m // tm, N_FF_TILES + 1),
        in_specs=[
            pl.BlockSpec(memory_space=pl.ANY),
            pl.BlockSpec((tm, D_MODEL), row),
            pl.BlockSpec((HALO, D_MODEL), lambda i, j: (jnp.maximum(i * nb - 1, 0), 0)),
            pl.BlockSpec((HALO, D_MODEL), lambda i, j: (jnp.minimum((i + 1) * nb, last), 0)),
            pl.BlockSpec((D_MODEL, FF_TILE), lambda i, j: (0, up_tile(j))),
            pl.BlockSpec((D_MODEL, FF_TILE), lambda i, j: (0, N_FF_TILES + up_tile(j))),
            pl.BlockSpec((None, 3, FF_TILE), gate),
            pl.BlockSpec((None, 3, FF_TILE), val),
            pl.BlockSpec((None, 1, FF_TILE), gate),
            pl.BlockSpec((None, 1, FF_TILE), val),
            pl.BlockSpec((FF_TILE, D_MODEL), lambda i, j: (jnp.maximum(j - 1, 0), 0)),
            pl.BlockSpec((1, D_MODEL), lambda i, j: (0, 0)),
        ],
        out_specs=out_specs,
        out_shape=out_shape,
        scratch_shapes=[
            pltpu.VMEM((tm + 2 * HALO, D_MODEL), _BF16),
            pltpu.VMEM((FF_TILE // LANES, tm + 2 * HALO, LANES), _F32),
            pltpu.VMEM((FF_TILE // LANES, tm + 2 * HALO, LANES), _F32),
            pltpu.VMEM((2, tm, FF_TILE), _BF16),
            pltpu.SemaphoreType.DMA(()),
        ],
        compiler_params=_params(("arbitrary", "arbitrary")),
        name="conv_ffn",
    )(x, h, h, h, w_up, w_up, conv_k, conv_k, conv_b, conv_b, w_down, g_next)


def _pad_ff(a):
    gate, val = jnp.split(a, 2, axis=-1)
    pad = [(0, 0)] * (a.ndim - 1) + [(0, D_FF_PAD - D_FF)]
    return jnp.concatenate([jnp.pad(gate, pad), jnp.pad(val, pad)], axis=-1)


def kernel(x, norm_mix, w_in, gm_ln_g, gm_ln_b, gm_w_s, gm_b_s, na_rpb, w_branch, w_out,
           norm_ffn, w_up, conv_k, conv_b, w_down, norm_final):
    b, t, d = x.shape
    assert (b, t, d) == (1, SEQ, D_MODEL)
    xs = x.reshape(t, d)
    bias = _na_bias(na_rpb.reshape(-1))
    conv_k_p, conv_b_p = _pad_ff(conv_k), _pad_ff(conv_b.reshape(DEPTH, 1, -1))
    w_s_b = gm_w_s.astype(_BF16)
    uv0, qkv0, gt0 = 0, 2 * GM_WIDTH, 2 * GM_WIDTH + 3 * NA_WIDTH
    n_uv = (2 * GM_WIDTH // PROJ_TN) * (t // PROJ_TM)
    n_gt = (2 * D_MODEL // PROJ_TN) * (t // PROJ_TM)
    uv_sides = (
        SideCast(_cast_down_rows, w_down, D_FF_PAD // n_uv, D_FF_PAD, d),
        SideCast(_cast_rows, w_branch, w_branch.shape[1] // n_uv, w_branch.shape[1], d),
        SideCast(_cast_rows, w_out, d // n_uv, d, d),
    )
    gt_sides = (SideCast(_cast_up_rows, w_up, d // n_gt, d, 2 * D_FF_PAD),)
    h = _rms_bf16(xs, norm_mix[0].reshape(1, d))
    for l in range(DEPTH):
        uv, w_down_b, w_branch_b, w_out_b = _proj(h, w_in, l, uv0, 2 * GM_WIDTH, "gelu",
                                                  "proj_uv", uv_sides)
        qkv, = _proj(h, w_in, l, qkv0, 3 * NA_WIDTH, "none", "proj_qkv")
        gates, w_up_b = _proj(h, w_in, l, gt0, 2 * D_MODEL, "sigmoid", "proj_gates", gt_sides)
        ya = _gmlp(uv, gm_ln_g[l].reshape(1, -1), gm_ln_b[l].reshape(1, -1),
                   w_s_b[l], gm_b_s[l].reshape(GM_GROUPS, CHUNK, 1))
        yb = _na(qkv, bias, l)
        xs, h = _merge(xs, ya, yb, gates, w_branch_b, w_out_b, norm_ffn[l].reshape(1, d))
        if l == DEPTH - 1:
            xs, = _ffn(xs, h, w_up_b, conv_k_p, conv_b_p, w_down_b, l,
                       norm_final.reshape(1, d), True)
        else:
            xs, h = _ffn(xs, h, w_up_b, conv_k_p, conv_b_p, w_down_b, l,
                         norm_mix[l + 1].reshape(1, d), False)
    return xs.reshape(b, t, d)
```

```python
from functools import partial
from typing import Callable, NamedTuple

import jax
import jax.numpy as jnp
from jax import lax
from jax.experimental import pallas as pl
from jax.experimental.pallas import tpu as pltpu

D_MODEL = 2048
SEQ = 8192
DEPTH = 2
GRID_W = 64
ROWS = SEQ // GRID_W
GM_GROUPS = 8
GM_HEAD = 128
GM_WIDTH = GM_GROUPS * GM_HEAD
CHUNK = 128
NA_HEADS = 16
NA_HEAD_DIM = 64
NA_WIDTH = NA_HEADS * NA_HEAD_DIM
NA_KR = 8
NA_KC = 16
D_FF = 5504
RMS_EPS = 1e-6
LN_EPS = 1e-5
NEG_INF = -1e30

LANES = 128
BF16_SUBLANES = 16
VMEM_BYTES = 64 * 1024 * 1024
VMEM_LIMIT_BYTES = VMEM_BYTES - 4 * 1024 * 1024

FF_TILE = 512
D_FF_PAD = ((D_FF + FF_TILE - 1) // FF_TILE) * FF_TILE
N_FF_TILES = D_FF_PAD // FF_TILE
PROJ_TM = 1024
PROJ_TN = 1024
MERGE_TM = 512
FFN_TM = 1024
HALO = BF16_SUBLANES
NA_KTOK = NA_KR * GRID_W
NA_ROWS_PER_STEP = 16
NA_UNROLL = 8
NA_CLASS_ROW = (0, 1, 2, 3, ROWS // 2, ROWS - 3, ROWS - 2, ROWS - 1)
N_CLASSES = len(NA_CLASS_ROW)
NA_INTERIOR_CLASS = NA_KR // 2

_BF16 = jnp.bfloat16
_F32 = jnp.float32


def _params(sem):
    return pltpu.CompilerParams(dimension_semantics=sem, vmem_limit_bytes=VMEM_LIMIT_BYTES)


def _rms(x, g):
    return x * lax.rsqrt(jnp.mean(x * x, axis=-1, keepdims=True) + RMS_EPS) * g


def _gelu_tanh(x):
    return 0.5 * x * (1.0 + jnp.tanh(0.7978845608028654 * (x + 0.044715 * (x * x * x))))


def _sigmoid(x):
    return 1.0 / (1.0 + jnp.exp(-x))


_ACTS = {"gelu": _gelu_tanh, "sigmoid": _sigmoid, "none": lambda x: x}


class SideCast(NamedTuple):
    body: Callable
    array: jax.Array
    rows: int
    out_rows: int
    out_cols: int


def _cast_rows(x_ref, o_ref, step):
    del step
    o_ref[...] = x_ref[...].astype(o_ref.dtype)


def _cast_up_rows(x_ref, o_ref, step):
    del step
    zeros = jnp.zeros((x_ref.shape[0], D_FF_PAD - D_FF), o_ref.dtype)
    o_ref[:, :D_FF] = x_ref[:, :D_FF].astype(o_ref.dtype)
    o_ref[:, D_FF:D_FF_PAD] = zeros
    o_ref[:, D_FF_PAD:D_FF_PAD + D_FF] = x_ref[:, D_FF:].astype(o_ref.dtype)
    o_ref[:, D_FF_PAD + D_FF:] = zeros


def _cast_down_rows(x_ref, o_ref, step):
    row = step * x_ref.shape[0] + lax.broadcasted_iota(jnp.int32, x_ref.shape, 0)
    o_ref[...] = jnp.where(row < D_FF, x_ref[...], 0.0).astype(o_ref.dtype)


def _proj_kernel(*refs, act, side_bodies):
    n = len(side_bodies)
    h_ref, w_ref = refs[:2]
    side_in = refs[2:2 + n]
    o_ref = refs[2 + n]
    side_out = refs[3 + n:3 + 2 * n]
    w_scr = refs[3 + 2 * n]

    @pl.when(pl.program_id(1) == 0)
    def _():
        w_scr[...] = w_ref[...].astype(_BF16)

    step = pl.program_id(0) * pl.num_programs(1) + pl.program_id(1)
    for body, x_ref, y_ref in zip(side_bodies, side_in, side_out):
        body(x_ref, y_ref, step)

    acc = jnp.dot(h_ref[...], w_scr[...], preferred_element_type=_F32)
    o_ref[...] = _ACTS[act](acc).astype(o_ref.dtype)


def _proj(h, w_in, layer, col0, ncols, act, name, sides=()):
    m = h.shape[0]
    jb0 = col0 // PROJ_TN
    n_i = m // PROJ_TM
    grid = (ncols // PROJ_TN, n_i)
    slab = lambda j, i: (j * n_i + i, 0)
    for s in sides:
        assert s.rows * grid[0] * grid[1] == s.out_rows, (name, s.rows, s.out_rows)
    outs = pl.pallas_call(
        partial(_proj_kernel, act=act, side_bodies=tuple(s.body for s in sides)),
        grid=grid,
        in_specs=[
            pl.BlockSpec((PROJ_TM, D_MODEL), lambda j, i: (i, 0)),
            pl.BlockSpec((None, D_MODEL, PROJ_TN), lambda j, i: (layer, 0, jb0 + j)),
        ] + [pl.BlockSpec((None, s.rows, s.array.shape[2]), lambda j, i: (layer, j * n_i + i, 0))
             for s in sides],
        out_specs=[pl.BlockSpec((PROJ_TM, PROJ_TN), lambda j, i: (i, j))]
        + [pl.BlockSpec((s.rows, s.out_cols), slab) for s in sides],
        out_shape=[jax.ShapeDtypeStruct((m, ncols), _BF16)]
        + [jax.ShapeDtypeStruct((s.out_rows, s.out_cols), _BF16) for s in sides],
        scratch_shapes=[pltpu.VMEM((D_MODEL, PROJ_TN), _BF16)],
        compiler_params=_params(("arbitrary", "arbitrary")),
        name=name,
    )(h, w_in, *[s.array for s in sides])
    return outs


def _gmlp_gate(uv_ref, lng_ref, lnb_ref, ws_ref, bs_ref, ya_ref):
    v = uv_ref[:, GM_WIDTH:].astype(_F32)
    mu = jnp.mean(v, axis=-1, keepdims=True)
    vc = v - mu
    var = jnp.mean(vc * vc, axis=-1, keepdims=True)
    vn = (vc * lax.rsqrt(var + LN_EPS) * lng_ref[...] + lnb_ref[...]).astype(_BF16)
    for n in range(uv_ref.shape[0] // CHUNK):
        rows = slice(n * CHUNK, (n + 1) * CHUNK)
        for g in range(GM_GROUPS):
            cols = slice(g * GM_HEAD, (g + 1) * GM_HEAD)
            mixed = jnp.dot(ws_ref[g], vn[rows, cols], preferred_element_type=_F32) + bs_ref[g]
            ya_ref[rows, cols] = (uv_ref[rows, cols].astype(_F32) * mixed).astype(ya_ref.dtype)


RPB_LANE0 = LANES // 2 - (NA_KC - 1)


def _prologue_kernel(rpb_ref, x_ref, g_ref, b_ref, h_ref):
    h_ref[...] = _rms(x_ref[...], g_ref[...]).astype(h_ref.dtype)

    c = lax.broadcasted_iota(jnp.int32, (GRID_W, LANES), 0)
    lane = lax.broadcasted_iota(jnp.int32, (GRID_W, LANES), 1)
    first = lane < GRID_W
    k = jnp.where(first, lane, lane - GRID_W)
    cs = jnp.clip(c - NA_KC // 2, 0, GRID_W - NA_KC)
    in_win = (k >= cs) & (k < cs + NA_KC)

    def toeplitz(dr, shift):
        row = jnp.broadcast_to(rpb_ref[0, dr:dr + 1, :], (GRID_W, LANES))
        return pltpu.roll(row, shift, axis=1, stride=1, stride_axis=0)

    n_dr = 2 * NA_KR - 1
    pairs = [jnp.where(in_win, jnp.where(first, toeplitz(dr, LANES // 2), toeplitz(dr + 1, 0)),
                       NEG_INF) for dr in range(n_dr - 1)]
    for cls, r in enumerate(NA_CLASS_ROW):
        rs = min(max(r - NA_KR // 2, 0), ROWS - NA_KR)
        for i in range(0, NA_KR, 2):
            b_ref[0, cls, :, i * GRID_W:(i + 2) * GRID_W] = pairs[rs + i - r + NA_KR - 1]


def _prologue(rpb, x, g):
    n = DEPTH * NA_HEADS
    m = x.shape[0]
    rows = m // n
    rpb_p = jnp.pad(rpb.reshape(n, 2 * NA_KR - 1, 2 * NA_KC - 1),
                    ((0, 0), (0, 1), (RPB_LANE0, LANES - RPB_LANE0 - (2 * NA_KC - 1))))
    return pl.pallas_call(
        _prologue_kernel,
        grid=(n,),
        in_specs=[pl.BlockSpec((1, 2 * NA_KR, LANES), lambda s: (s, 0, 0)),
                  pl.BlockSpec((rows, D_MODEL), lambda s: (s, 0)),
                  pl.BlockSpec((1, D_MODEL), lambda s: (0, 0))],
        out_specs=[pl.BlockSpec((1, N_CLASSES, GRID_W, NA_KTOK), lambda s: (s // 2, 0, s % 2, 0)),
                   pl.BlockSpec((rows, D_MODEL), lambda s: (s, 0))],
        out_shape=[jax.ShapeDtypeStruct((n // 2, N_CLASSES, 2 * GRID_W, NA_KTOK), _F32),
                   jax.ShapeDtypeStruct((m, D_MODEL), _BF16)],
        compiler_params=_params(("arbitrary",)),
        name="prologue",
    )(rpb_p, x, g)


def _na_window(r):
    rs = jnp.clip(r - NA_KR // 2, 0, ROWS - NA_KR)
    cls = jnp.where(r < NA_INTERIOR_CLASS, r,
                    jnp.where(r > ROWS - NA_KR // 2, r - (ROWS - NA_KR), NA_INTERIOR_CLASS))
    return pl.ds(pl.multiple_of(rs * GRID_W, GRID_W), NA_KTOK), cls


def _na_kernel(q_ref, k_ref, v_ref, b_ref, o_ref, p_scr, inv_scr):
    rb = pl.program_id(1)
    n_rb = pl.num_programs(1) - 1
    scale = NA_HEAD_DIM ** -0.5
    head0 = lax.broadcasted_iota(jnp.int32, (GRID_W, LANES), 1) < NA_HEAD_DIM

    def run(scores, apply):
        def group(g, carry):
            for u in range(NA_UNROLL):
                t = g * NA_UNROLL + u
                qrows = pl.ds(pl.multiple_of(t * GRID_W, GRID_W), GRID_W)
                if scores:
                    krows, cls = _na_window(rb * NA_ROWS_PER_STEP + t)
                    q = q_ref[qrows, :] * scale
                    zero = jnp.zeros_like(q)
                    q2 = jnp.concatenate([jnp.where(head0, q, zero), jnp.where(head0, zero, q)],
                                         axis=0)
                    s = lax.dot_general(q2, k_ref[krows, :], (((1,), (1,)), ((), ())),
                                        preferred_element_type=_F32) + b_ref[0, cls]
                    p = jnp.exp(s - jnp.max(s, axis=-1, keepdims=True))
                    den = jnp.sum(p, axis=-1, keepdims=True)
                if apply:
                    vrows, _ = _na_window((rb - 1) * NA_ROWS_PER_STEP + t)
                    o = jnp.dot(p_scr[t], v_ref[vrows, :],
                                preferred_element_type=_F32) * inv_scr[t]
                    o_ref[qrows, :] = jnp.where(head0, o[:GRID_W], o[GRID_W:]).astype(o_ref.dtype)
                if scores:
                    p_scr[t] = p.astype(_BF16)
                    inv_scr[t] = jnp.broadcast_to(1.0 / den, (2 * GRID_W, LANES))
            return carry

        lax.fori_loop(0, NA_ROWS_PER_STEP // NA_UNROLL, group, 0)

    pl.when(rb == 0)(lambda: run(True, False))
    pl.when((rb > 0) & (rb < n_rb))(lambda: run(True, True))
    pl.when(rb == n_rb)(lambda: run(False, True))


def _na(qkv, bias, layer):
    m = qkv.shape[0]
    tq = NA_ROWS_PER_STEP * GRID_W
    n_hp = NA_WIDTH // LANES
    n_rb = m // tq
    return pl.pallas_call(
        _na_kernel,
        grid=(n_hp, n_rb + 1),
        in_specs=[
            pl.BlockSpec((tq, LANES), lambda hp, rb: (jnp.minimum(rb, n_rb - 1), hp)),
            pl.BlockSpec((m, LANES), lambda hp, rb: (0, n_hp + hp)),
            pl.BlockSpec((m, LANES), lambda hp, rb: (0, 2 * n_hp + hp)),
            pl.BlockSpec((1, N_CLASSES, 2 * GRID_W, NA_KTOK),
                         lambda hp, rb: (layer * n_hp + hp, 0, 0, 0)),
        ],
        out_specs=pl.BlockSpec((tq, LANES), lambda hp, rb: (jnp.maximum(rb - 1, 0), hp)),
        out_shape=jax.ShapeDtypeStruct((m, NA_WIDTH), _BF16),
        scratch_shapes=[
            pltpu.VMEM((NA_ROWS_PER_STEP, 2 * GRID_W, NA_KTOK), _BF16),
            pltpu.VMEM((NA_ROWS_PER_STEP, 2 * GRID_W, LANES), _F32),
        ],
        compiler_params=_params(("arbitrary", "arbitrary")),
        name="na_attn",
    )(qkv, qkv, qkv, bias)


def _merge_kernel(x_ref, uv_ref, lng_ref, lnb_ref, ws_ref, bs_ref, yb_ref, gt_ref, wb_ref, wo_ref,
                  g_ref, o_ref, h_ref, ya_scr):
    _gmlp_gate(uv_ref, lng_ref, lnb_ref, ws_ref, bs_ref, ya_scr)
    ta = jnp.dot(ya_scr[...], wb_ref[:GM_WIDTH, :], preferred_element_type=_F32)
    tb = jnp.dot(yb_ref[...], wb_ref[GM_WIDTH:, :], preferred_element_type=_F32)
    merged = (gt_ref[:, :D_MODEL].astype(_F32) * ta + gt_ref[:, D_MODEL:].astype(_F32) * tb)
    y = x_ref[...] + jnp.dot(merged.astype(_BF16), wo_ref[...], preferred_element_type=_F32)
    o_ref[...] = y
    h_ref[...] = _rms(y, g_ref[...]).astype(h_ref.dtype)


def _merge(x, uv, ln_g, ln_b, w_s, b_s, yb, gates, w_branch, w_out, g_ffn):
    m = x.shape[0]
    resident = dict(pipeline_mode=pl.Buffered(1))
    row = lambda i: (i, 0)
    fixed2 = lambda i: (0, 0)
    fixed3 = lambda i: (0, 0, 0)
    return pl.pallas_call(
        _merge_kernel,
        grid=(m // MERGE_TM,),
        in_specs=[
            pl.BlockSpec((MERGE_TM, D_MODEL), row),
            pl.BlockSpec((MERGE_TM, 2 * GM_WIDTH), row),
            pl.BlockSpec((1, GM_WIDTH), fixed2),
            pl.BlockSpec((1, GM_WIDTH), fixed2),
            pl.BlockSpec((GM_GROUPS, CHUNK, CHUNK), fixed3),
            pl.BlockSpec((GM_GROUPS, CHUNK, 1), fixed3),
            pl.BlockSpec((MERGE_TM, NA_WIDTH), row),
            pl.BlockSpec((MERGE_TM, 2 * D_MODEL), row),
            pl.BlockSpec((GM_WIDTH + NA_WIDTH, D_MODEL), fixed2, **resident),
            pl.BlockSpec((D_MODEL, D_MODEL), fixed2, **resident),
            pl.BlockSpec((1, D_MODEL), fixed2),
        ],
        out_specs=[pl.BlockSpec((MERGE_TM, D_MODEL), row), pl.BlockSpec((MERGE_TM, D_MODEL), row)],
        out_shape=[jax.ShapeDtypeStruct((m, D_MODEL), _F32),
                   jax.ShapeDtypeStruct((m, D_MODEL), _BF16)],
        scratch_shapes=[pltpu.VMEM((MERGE_TM, GM_WIDTH), _BF16)],
        compiler_params=_params(("arbitrary",)),
        name="merge_out",
    )(x, uv, ln_g, ln_b, w_s, b_s, yb, gates, w_branch, w_out, g_ffn)


def _ffn_kernel(x_hbm, h_ref, hp_ref, hn_ref, wg_ref, wv_ref, kg_ref, kv_ref, bg_ref, bv_ref,
                wd_ref, gn_ref, *refs, final_norm):
    if final_norm:
        o_ref, h_scr, ug_scr, uv_scr, act_scr, x_sem = refs
    else:
        o_ref, ho_ref, h_scr, ug_scr, uv_scr, act_scr, x_sem = refs
    i, j = pl.program_id(0), pl.program_id(1)
    n_ff = pl.num_programs(1) - 1
    tm = FFN_TM

    def x_copy():
        rows = pl.ds(pl.multiple_of(i * tm, tm), tm)
        return pltpu.make_async_copy(x_hbm.at[rows, :], o_ref, x_sem)

    def up_dots():
        h = h_scr[...]
        for u_scr, w_ref in ((ug_scr, wg_ref), (uv_scr, wv_ref)):
            u = jnp.dot(h, w_ref[...], preferred_element_type=_F32)
            for c in range(FF_TILE // LANES):
                u_scr[c] = u[:, c * LANES:(c + 1) * LANES]

    def conv(u_scr, k_ref, b_ref, c):
        cols = slice(c * LANES, (c + 1) * LANES)
        return (u_scr[c, HALO - 1:HALO - 1 + tm, :] * k_ref[0:1, cols]
                + u_scr[c, HALO:HALO + tm, :] * k_ref[1:2, cols]
                + u_scr[c, HALO + 1:HALO + 1 + tm, :] * k_ref[2:3, cols] + b_ref[:, cols])

    def gate_into(slot):
        for c in range(FF_TILE // LANES):
            gate = conv(ug_scr, kg_ref, bg_ref, c)
            val = conv(uv_scr, kv_ref, bv_ref, c)
            act_scr[slot, :, c * LANES:(c + 1) * LANES] = (
                gate * _sigmoid(gate) * val).astype(_BF16)

    def down_from(slot):
        o_ref[...] += jnp.dot(act_scr[slot], wd_ref[...], preferred_element_type=_F32)

    @pl.when(j == 0)
    def _():
        zero = jnp.zeros((HALO, D_MODEL), _BF16)
        h_scr[:HALO, :] = jnp.where(i > 0, hp_ref[...], zero)
        h_scr[HALO:HALO + tm, :] = h_ref[...]
        h_scr[HALO + tm:, :] = jnp.where(i < pl.num_programs(0) - 1, hn_ref[...], zero)
        x_copy().start()
        up_dots()
        gate_into(0)

    @pl.when(j == 1)
    def _():
        x_copy().wait()

    for parity in range(2):
        @pl.when((j > 0) & (j < n_ff) & (j % 2 == parity))
        def _():
            up_dots()
            down_from(1 - parity)
            gate_into(parity)

    @pl.when(j == n_ff)
    def _():
        down_from((N_FF_TILES - 1) % 2)
        hn = _rms(o_ref[...], gn_ref[...])
        if final_norm:
            o_ref[...] = hn
        else:
            ho_ref[...] = hn.astype(ho_ref.dtype)


def _ffn(x, h, w_up, conv_k, conv_b, w_down, layer, g_next, final_norm):
    m = x.shape[0]
    tm = FFN_TM
    nb = tm // HALO
    last = m // HALO - 1
    row = lambda i, j: (i, 0)
    up_tile = lambda j: jnp.minimum(j, N_FF_TILES - 1)
    gate = lambda i, j: (layer, 0, up_tile(j))
    val = lambda i, j: (layer, 0, N_FF_TILES + up_tile(j))
    out_specs = [pl.BlockSpec((tm, D_MODEL), row)]
    out_shape = [jax.ShapeDtypeStruct((m, D_MODEL), _F32)]
    if not final_norm:
        out_specs.append(pl.BlockSpec((tm, D_MODEL), row))
        out_shape.append(jax.ShapeDtypeStruct((m, D_MODEL), _BF16))
    return pl.pallas_call(
        partial(_ffn_kernel, final_norm=final_norm),
        grid=(m // tm, N_FF_TILES + 1),
        in_specs=[
            pl.BlockSpec(memory_space=pl.ANY),
            pl.BlockSpec((tm, D_MODEL), row),
            pl.BlockSpec((HALO, D_MODEL), lambda i, j: (jnp.maximum(i * nb - 1, 0), 0)),
            pl.BlockSpec((HALO, D_MODEL), lambda i, j: (jnp.minimum((i + 1) * nb, last), 0)),
            pl.BlockSpec((D_MODEL, FF_TILE), lambda i, j: (0, up_tile(j))),
            pl.BlockSpec((D_MODEL, FF_TILE), lambda i, j: (0, N_FF_TILES + up_tile(j))),
            pl.BlockSpec((None, 3, FF_TILE), gate),
            pl.BlockSpec((None, 3, FF_TILE), val),
            pl.BlockSpec((None, 1, FF_TILE), gate),
            pl.BlockSpec((None, 1, FF_TILE), val),
            pl.BlockSpec((FF_TILE, D_MODEL), lambda i, j: (jnp.maximum(j - 1, 0), 0)),
            pl.BlockSpec((1, D_MODEL), lambda i, j: (0, 0)),
        ],
        out_specs=out_specs,
        out_shape=out_shape,
        scratch_shapes=[
            pltpu.VMEM((tm + 2 * HALO, D_MODEL), _BF16),
            pltpu.VMEM((FF_TILE // LANES, tm + 2 * HALO, LANES), _F32),
            pltpu.VMEM((FF_TILE // LANES, tm + 2 * HALO, LANES), _F32),
            pltpu.VMEM((2, tm, FF_TILE), _BF16),
            pltpu.SemaphoreType.DMA(()),
        ],
        compiler_params=_params(("arbitrary", "arbitrary")),
        name="conv_ffn",
    )(x, h, h, h, w_up, w_up, conv_k, conv_k, conv_b, conv_b, w_down, g_next)


def _pad_ff(a):
    gate, val = jnp.split(a, 2, axis=-1)
    pad = [(0, 0)] * (a.ndim - 1) + [(0, D_FF_PAD - D_FF)]
    return jnp.concatenate([jnp.pad(gate, pad), jnp.pad(val, pad)], axis=-1)


def kernel(x, norm_mix, w_in, gm_ln_g, gm_ln_b, gm_w_s, gm_b_s, na_rpb, w_branch, w_out,
           norm_ffn, w_up, conv_k, conv_b, w_down, norm_final):
    b, t, d = x.shape
    assert (b, t, d) == (1, SEQ, D_MODEL)
    xs = x.reshape(t, d)
    bias, h = _prologue(na_rpb, xs, norm_mix[0].reshape(1, d))
    conv_k_p, conv_b_p = _pad_ff(conv_k), _pad_ff(conv_b.reshape(DEPTH, 1, -1))
    w_s_b = gm_w_s.astype(_BF16)
    uv0, qkv0, gt0 = 0, 2 * GM_WIDTH, 2 * GM_WIDTH + 3 * NA_WIDTH
    n_uv = (2 * GM_WIDTH // PROJ_TN) * (t // PROJ_TM)
    n_gt = (2 * D_MODEL // PROJ_TN) * (t // PROJ_TM)
    uv_sides = (
        SideCast(_cast_down_rows, w_down, D_FF_PAD // n_uv, D_FF_PAD, d),
        SideCast(_cast_rows, w_branch, w_branch.shape[1] // n_uv, w_branch.shape[1], d),
        SideCast(_cast_rows, w_out, d // n_uv, d, d),
    )
    gt_sides = (SideCast(_cast_up_rows, w_up, d // n_gt, d, 2 * D_FF_PAD),)
    for l in range(DEPTH):
        uv, w_down_b, w_branch_b, w_out_b = _proj(h, w_in, l, uv0, 2 * GM_WIDTH, "gelu",
                                                  "proj_uv", uv_sides)
        qkv, = _proj(h, w_in, l, qkv0, 3 * NA_WIDTH, "none", "proj_qkv")
        gates, w_up_b = _proj(h, w_in, l, gt0, 2 * D_MODEL, "sigmoid", "proj_gates", gt_sides)
        yb = _na(qkv, bias, l)
        xs, h = _merge(xs, uv, gm_ln_g[l].reshape(1, -1), gm_ln_b[l].reshape(1, -1), w_s_b[l],
                       gm_b_s[l].reshape(GM_GROUPS, CHUNK, 1), yb, gates, w_branch_b, w_out_b,
                       norm_ffn[l].reshape(1, d))
        if l == DEPTH - 1:
            xs, = _ffn(xs, h, w_up_b, conv_k_p, conv_b_p, w_down_b, l,
                       norm_final.reshape(1, d), True)
        else:
            xs, h = _ffn(xs, h, w_up_b, conv_k_p, conv_b_p, w_down_b, l,
                         norm_mix[l + 1].reshape(1, d), False)
    return xs.reshape(b, t, d)
```

```python
from functools import partial
from typing import Callable, NamedTuple

import jax
import jax.numpy as jnp
from jax import lax
from jax.experimental import pallas as pl
from jax.experimental.pallas import tpu as pltpu

D_MODEL = 2048
SEQ = 8192
DEPTH = 2
GRID_W = 64
ROWS = SEQ // GRID_W
GM_GROUPS = 8
GM_HEAD = 128
GM_WIDTH = GM_GROUPS * GM_HEAD
CHUNK = 128
NA_HEADS = 16
NA_HEAD_DIM = 64
NA_WIDTH = NA_HEADS * NA_HEAD_DIM
NA_KR = 8
NA_KC = 16
D_FF = 5504
RMS_EPS = 1e-6
LN_EPS = 1e-5
NEG_INF = -1e30

LANES = 128
BF16_SUBLANES = 16
VMEM_BYTES = 64 * 1024 * 1024
VMEM_LIMIT_BYTES = VMEM_BYTES - 2 * 1024 * 1024

FF_TILE = 512
D_FF_PAD = ((D_FF + FF_TILE - 1) // FF_TILE) * FF_TILE
N_FF_TILES = D_FF_PAD // FF_TILE
PROJ_TM = 1024
PROJ_TN = 1024
MERGE_TM = 512
FFN_TM = 1024
FFN_LAG = 2
HALO = BF16_SUBLANES
NA_KTOK = NA_KR * GRID_W
NA_ROWS_PER_STEP = 16
NA_UNROLL = 8
NA_CLASS_ROW = (0, 1, 2, 3, ROWS // 2, ROWS - 3, ROWS - 2, ROWS - 1)
N_CLASSES = len(NA_CLASS_ROW)
NA_INTERIOR_CLASS = NA_KR // 2

_BF16 = jnp.bfloat16
_F32 = jnp.float32


def _params(sem):
    return pltpu.CompilerParams(dimension_semantics=sem, vmem_limit_bytes=VMEM_LIMIT_BYTES)


def _rms(x, g):
    return x * lax.rsqrt(jnp.mean(x * x, axis=-1, keepdims=True) + RMS_EPS) * g


def _gelu_tanh(x):
    return 0.5 * x * (1.0 + jnp.tanh(0.7978845608028654 * (x + 0.044715 * (x * x * x))))


def _sigmoid(x):
    return 1.0 / (1.0 + jnp.exp(-x))


_ACTS = {"gelu": _gelu_tanh, "sigmoid": _sigmoid, "none": lambda x: x}


class SideCast(NamedTuple):
    body: Callable
    array: jax.Array
    rows: int
    out_rows: int
    out_cols: int


def _cast_rows(x_ref, o_ref, step):
    del step
    o_ref[...] = x_ref[...].astype(o_ref.dtype)


def _cast_up_rows(x_ref, o_ref, step):
    del step
    zeros = jnp.zeros((x_ref.shape[0], D_FF_PAD - D_FF), o_ref.dtype)
    o_ref[:, :D_FF] = x_ref[:, :D_FF].astype(o_ref.dtype)
    o_ref[:, D_FF:D_FF_PAD] = zeros
    o_ref[:, D_FF_PAD:D_FF_PAD + D_FF] = x_ref[:, D_FF:].astype(o_ref.dtype)
    o_ref[:, D_FF_PAD + D_FF:] = zeros


def _cast_down_rows(x_ref, o_ref, step):
    row = step * x_ref.shape[0] + lax.broadcasted_iota(jnp.int32, x_ref.shape, 0)
    o_ref[...] = jnp.where(row < D_FF, x_ref[...], 0.0).astype(o_ref.dtype)


def _proj_kernel(*refs, act, side_bodies):
    n = len(side_bodies)
    h_ref, w_ref = refs[:2]
    side_in = refs[2:2 + n]
    o_ref = refs[2 + n]
    side_out = refs[3 + n:3 + 2 * n]
    w_scr = refs[3 + 2 * n]

    @pl.when(pl.program_id(1) == 0)
    def _():
        w_scr[...] = w_ref[...].astype(_BF16)

    step = pl.program_id(0) * pl.num_programs(1) + pl.program_id(1)
    for body, x_ref, y_ref in zip(side_bodies, side_in, side_out):
        body(x_ref, y_ref, step)

    acc = jnp.dot(h_ref[...], w_scr[...], preferred_element_type=_F32)
    o_ref[...] = _ACTS[act](acc).astype(o_ref.dtype)


def _proj(h, w_in, layer, col0, ncols, act, name, sides=()):
    m = h.shape[0]
    jb0 = col0 // PROJ_TN
    n_i = m // PROJ_TM
    grid = (ncols // PROJ_TN, n_i)
    slab = lambda j, i: (j * n_i + i, 0)
    for s in sides:
        assert s.rows * grid[0] * grid[1] == s.out_rows, (name, s.rows, s.out_rows)
    outs = pl.pallas_call(
        partial(_proj_kernel, act=act, side_bodies=tuple(s.body for s in sides)),
        grid=grid,
        in_specs=[
            pl.BlockSpec((PROJ_TM, D_MODEL), lambda j, i: (i, 0)),
            pl.BlockSpec((None, D_MODEL, PROJ_TN), lambda j, i: (layer, 0, jb0 + j)),
        ] + [pl.BlockSpec((None, s.rows, s.array.shape[2]), lambda j, i: (layer, j * n_i + i, 0))
             for s in sides],
        out_specs=[pl.BlockSpec((PROJ_TM, PROJ_TN), lambda j, i: (i, j))]
        + [pl.BlockSpec((s.rows, s.out_cols), slab) for s in sides],
        out_shape=[jax.ShapeDtypeStruct((m, ncols), _BF16)]
        + [jax.ShapeDtypeStruct((s.out_rows, s.out_cols), _BF16) for s in sides],
        scratch_shapes=[pltpu.VMEM((D_MODEL, PROJ_TN), _BF16)],
        compiler_params=_params(("arbitrary", "arbitrary")),
        name=name,
    )(h, w_in, *[s.array for s in sides])
    return outs


def _gmlp_gate(uv_ref, lng_ref, lnb_ref, ws_ref, bs_ref, ya_ref):
    v = uv_ref[:, GM_WIDTH:].astype(_F32)
    mu = jnp.mean(v, axis=-1, keepdims=True)
    vc = v - mu
    var = jnp.mean(vc * vc, axis=-1, keepdims=True)
    vn = (vc * lax.rsqrt(var + LN_EPS) * lng_ref[...] + lnb_ref[...]).astype(_BF16)
    for n in range(uv_ref.shape[0] // CHUNK):
        rows = slice(n * CHUNK, (n + 1) * CHUNK)
        for g in range(GM_GROUPS):
            cols = slice(g * GM_HEAD, (g + 1) * GM_HEAD)
            mixed = jnp.dot(ws_ref[g], vn[rows, cols], preferred_element_type=_F32) + bs_ref[g]
            ya_ref[rows, cols] = (uv_ref[rows, cols].astype(_F32) * mixed).astype(ya_ref.dtype)


RPB_LANE0 = LANES // 2 - (NA_KC - 1)


def _prologue_kernel(rpb_ref, x_ref, g_ref, b_ref, h_ref):
    h_ref[...] = _rms(x_ref[...], g_ref[...]).astype(h_ref.dtype)

    c = lax.broadcasted_iota(jnp.int32, (GRID_W, LANES), 0)
    lane = lax.broadcasted_iota(jnp.int32, (GRID_W, LANES), 1)
    first = lane < GRID_W
    k = jnp.where(first, lane, lane - GRID_W)
    cs = jnp.clip(c - NA_KC // 2, 0, GRID_W - NA_KC)
    in_win = (k >= cs) & (k < cs + NA_KC)

    def toeplitz(dr, shift):
        row = jnp.broadcast_to(rpb_ref[0, dr:dr + 1, :], (GRID_W, LANES))
        return pltpu.roll(row, shift, axis=1, stride=1, stride_axis=0)

    n_dr = 2 * NA_KR - 1
    pairs = [jnp.where(in_win, jnp.where(first, toeplitz(dr, LANES // 2), toeplitz(dr + 1, 0)),
                       NEG_INF) for dr in range(n_dr - 1)]
    for cls, r in enumerate(NA_CLASS_ROW):
        rs = min(max(r - NA_KR // 2, 0), ROWS - NA_KR)
        for i in range(0, NA_KR, 2):
            b_ref[0, cls, :, i * GRID_W:(i + 2) * GRID_W] = pairs[rs + i - r + NA_KR - 1]


def _prologue(rpb, x, g):
    n = DEPTH * NA_HEADS
    m = x.shape[0]
    rows = m // n
    rpb_p = jnp.pad(rpb.reshape(n, 2 * NA_KR - 1, 2 * NA_KC - 1),
                    ((0, 0), (0, 1), (RPB_LANE0, LANES - RPB_LANE0 - (2 * NA_KC - 1))))
    return pl.pallas_call(
        _prologue_kernel,
        grid=(n,),
        in_specs=[pl.BlockSpec((1, 2 * NA_KR, LANES), lambda s: (s, 0, 0)),
                  pl.BlockSpec((rows, D_MODEL), lambda s: (s, 0)),
                  pl.BlockSpec((1, D_MODEL), lambda s: (0, 0))],
        out_specs=[pl.BlockSpec((1, N_CLASSES, GRID_W, NA_KTOK), lambda s: (s // 2, 0, s % 2, 0)),
                   pl.BlockSpec((rows, D_MODEL), lambda s: (s, 0))],
        out_shape=[jax.ShapeDtypeStruct((n // 2, N_CLASSES, 2 * GRID_W, NA_KTOK), _F32),
                   jax.ShapeDtypeStruct((m, D_MODEL), _BF16)],
        compiler_params=_params(("arbitrary",)),
        name="prologue",
    )(rpb_p, x, g)


def _na_window(r):
    rs = jnp.clip(r - NA_KR // 2, 0, ROWS - NA_KR)
    cls = jnp.where(r < NA_INTERIOR_CLASS, r,
                    jnp.where(r > ROWS - NA_KR // 2, r - (ROWS - NA_KR), NA_INTERIOR_CLASS))
    return pl.ds(pl.multiple_of(rs * GRID_W, GRID_W), NA_KTOK), cls


def _na_kernel(q_ref, k_ref, v_ref, b_ref, o_ref, p_scr, inv_scr):
    rb = pl.program_id(1)
    n_rb = pl.num_programs(1) - 1
    scale = NA_HEAD_DIM ** -0.5
    head0 = lax.broadcasted_iota(jnp.int32, (GRID_W, LANES), 1) < NA_HEAD_DIM

    def run(scores, apply):
        def group(g, carry):
            for u in range(NA_UNROLL):
                t = g * NA_UNROLL + u
                qrows = pl.ds(pl.multiple_of(t * GRID_W, GRID_W), GRID_W)
                if scores:
                    krows, cls = _na_window(rb * NA_ROWS_PER_STEP + t)
                    q = q_ref[qrows, :] * scale
                    zero = jnp.zeros_like(q)
                    q2 = jnp.concatenate([jnp.where(head0, q, zero), jnp.where(head0, zero, q)],
                                         axis=0)
                    s = lax.dot_general(q2, k_ref[krows, :], (((1,), (1,)), ((), ())),
                                        preferred_element_type=_F32) + b_ref[0, cls]
                    p = jnp.exp(s - jnp.max(s, axis=-1, keepdims=True))
                    den = jnp.sum(p, axis=-1, keepdims=True)
                if apply:
                    vrows, _ = _na_window((rb - 1) * NA_ROWS_PER_STEP + t)
                    o = jnp.dot(p_scr[t], v_ref[vrows, :],
                                preferred_element_type=_F32) * inv_scr[t]
                    o_ref[qrows, :] = jnp.where(head0, o[:GRID_W], o[GRID_W:]).astype(o_ref.dtype)
                if scores:
                    p_scr[t] = p.astype(_BF16)
                    inv_scr[t] = jnp.broadcast_to(1.0 / den, (2 * GRID_W, LANES))
            return carry

        lax.fori_loop(0, NA_ROWS_PER_STEP // NA_UNROLL, group, 0)

    pl.when(rb == 0)(lambda: run(True, False))
    pl.when((rb > 0) & (rb < n_rb))(lambda: run(True, True))
    pl.when(rb == n_rb)(lambda: run(False, True))


def _na(qkv, bias, layer):
    m = qkv.shape[0]
    tq = NA_ROWS_PER_STEP * GRID_W
    n_hp = NA_WIDTH // LANES
    n_rb = m // tq
    return pl.pallas_call(
        _na_kernel,
        grid=(n_hp, n_rb + 1),
        in_specs=[
            pl.BlockSpec((tq, LANES), lambda hp, rb: (jnp.minimum(rb, n_rb - 1), hp)),
            pl.BlockSpec((m, LANES), lambda hp, rb: (0, n_hp + hp)),
            pl.BlockSpec((m, LANES), lambda hp, rb: (0, 2 * n_hp + hp)),
            pl.BlockSpec((1, N_CLASSES, 2 * GRID_W, NA_KTOK),
                         lambda hp, rb: (layer * n_hp + hp, 0, 0, 0)),
        ],
        out_specs=pl.BlockSpec((tq, LANES), lambda hp, rb: (jnp.maximum(rb - 1, 0), hp)),
        out_shape=jax.ShapeDtypeStruct((m, NA_WIDTH), _BF16),
        scratch_shapes=[
            pltpu.VMEM((NA_ROWS_PER_STEP, 2 * GRID_W, NA_KTOK), _BF16),
            pltpu.VMEM((NA_ROWS_PER_STEP, 2 * GRID_W, LANES), _F32),
        ],
        compiler_params=_params(("arbitrary", "arbitrary")),
        name="na_attn",
    )(qkv, qkv, qkv, bias)


def _merge_kernel(x_ref, uv_ref, lng_ref, lnb_ref, ws_ref, bs_ref, yb_ref, gt_ref, wb_ref, wo_ref,
                  g_ref, o_ref, h_ref, ya_scr):
    _gmlp_gate(uv_ref, lng_ref, lnb_ref, ws_ref, bs_ref, ya_scr)
    ta = jnp.dot(ya_scr[...], wb_ref[:GM_WIDTH, :], preferred_element_type=_F32)
    tb = jnp.dot(yb_ref[...], wb_ref[GM_WIDTH:, :], preferred_element_type=_F32)
    merged = (gt_ref[:, :D_MODEL].astype(_F32) * ta + gt_ref[:, D_MODEL:].astype(_F32) * tb)
    y = x_ref[...] + jnp.dot(merged.astype(_BF16), wo_ref[...], preferred_element_type=_F32)
    o_ref[...] = y
    h_ref[...] = _rms(y, g_ref[...]).astype(h_ref.dtype)


def _merge(x, uv, ln_g, ln_b, w_s, b_s, yb, gates, w_branch, w_out, g_ffn):
    m = x.shape[0]
    resident = dict(pipeline_mode=pl.Buffered(1))
    row = lambda i: (i, 0)
    fixed2 = lambda i: (0, 0)
    fixed3 = lambda i: (0, 0, 0)
    return pl.pallas_call(
        _merge_kernel,
        grid=(m // MERGE_TM,),
        in_specs=[
            pl.BlockSpec((MERGE_TM, D_MODEL), row),
            pl.BlockSpec((MERGE_TM, 2 * GM_WIDTH), row),
            pl.BlockSpec((1, GM_WIDTH), fixed2),
            pl.BlockSpec((1, GM_WIDTH), fixed2),
            pl.BlockSpec((GM_GROUPS, CHUNK, CHUNK), fixed3),
            pl.BlockSpec((GM_GROUPS, CHUNK, 1), fixed3),
            pl.BlockSpec((MERGE_TM, NA_WIDTH), row),
            pl.BlockSpec((MERGE_TM, 2 * D_MODEL), row),
            pl.BlockSpec((GM_WIDTH + NA_WIDTH, D_MODEL), fixed2, **resident),
            pl.BlockSpec((D_MODEL, D_MODEL), fixed2, **resident),
            pl.BlockSpec((1, D_MODEL), fixed2),
        ],
        out_specs=[pl.BlockSpec((MERGE_TM, D_MODEL), row), pl.BlockSpec((MERGE_TM, D_MODEL), row)],
        out_shape=[jax.ShapeDtypeStruct((m, D_MODEL), _F32),
                   jax.ShapeDtypeStruct((m, D_MODEL), _BF16)],
        scratch_shapes=[pltpu.VMEM((MERGE_TM, GM_WIDTH), _BF16)],
        compiler_params=_params(("arbitrary",)),
        name="merge_out",
    )(x, uv, ln_g, ln_b, w_s, b_s, yb, gates, w_branch, w_out, g_ffn)


def _ffn_kernel(x_hbm, h_ref, hp_ref, hn_ref, wg_ref, wv_ref, kg_ref, kv_ref, bg_ref, bv_ref,
                wd_ref, gn_ref, *refs, final_norm):
    if final_norm:
        o_ref, h_scr, ug_scr, uv_scr, act_scr, x_sem = refs
    else:
        o_ref, ho_ref, h_scr, ug_scr, uv_scr, act_scr, x_sem = refs
    i, j = pl.program_id(0), pl.program_id(1)
    tm = FFN_TM

    def x_copy():
        rows = pl.ds(pl.multiple_of(i * tm, tm), tm)
        return pltpu.make_async_copy(x_hbm.at[rows, :], o_ref, x_sem)

    def up_dots():
        h = h_scr[...]
        for u_scr, w_ref in ((ug_scr, wg_ref), (uv_scr, wv_ref)):
            u = jnp.dot(h, w_ref[...], preferred_element_type=_F32)
            for c in range(FF_TILE // LANES):
                u_scr[c] = u[:, c * LANES:(c + 1) * LANES]

    def conv(u_scr, k_ref, b_ref, c):
        cols = slice(c * LANES, (c + 1) * LANES)
        return (u_scr[c, HALO - 1:HALO - 1 + tm, :] * k_ref[0:1, cols]
                + u_scr[c, HALO:HALO + tm, :] * k_ref[1:2, cols]
                + u_scr[c, HALO + 1:HALO + 1 + tm, :] * k_ref[2:3, cols] + b_ref[:, cols])

    def gate_into(slot):
        for c in range(FF_TILE // LANES):
            gate = conv(ug_scr, kg_ref, bg_ref, c)
            val = conv(uv_scr, kv_ref, bv_ref, c)
            act_scr[slot, :, c * LANES:(c + 1) * LANES] = (
                gate * _sigmoid(gate) * val).astype(_BF16)

    def down_from(slot):
        o_ref[...] += jnp.dot(act_scr[slot], wd_ref[...], preferred_element_type=_F32)

    @pl.when(j == 0)
    def _():
        zero = jnp.zeros((HALO, D_MODEL), _BF16)
        h_scr[:HALO, :] = jnp.where(i > 0, hp_ref[...], zero)
        h_scr[HALO:HALO + tm, :] = h_ref[...]
        h_scr[HALO + tm:, :] = jnp.where(i < pl.num_programs(0) - 1, hn_ref[...], zero)

    slots = FFN_LAG + 1
    for s in range(FFN_LAG):
        @pl.when(j == s)
        def _(s=s):
            if s == FFN_LAG - 1:
                x_copy().start()
            up_dots()
            gate_into(s)

    @pl.when(j == FFN_LAG)
    def _():
        x_copy().wait()

    for r in range(slots):
        @pl.when((j >= FFN_LAG) & (j < N_FF_TILES) & (j % slots == r))
        def _(r=r):
            up_dots()
            down_from((r - FFN_LAG) % slots)
            gate_into(r)

    for s in range(N_FF_TILES, N_FF_TILES + FFN_LAG):
        @pl.when(j == s)
        def _(s=s):
            down_from((s - FFN_LAG) % slots)
            if s == N_FF_TILES + FFN_LAG - 1:
                hn = _rms(o_ref[...], gn_ref[...])
                if final_norm:
                    o_ref[...] = hn
                else:
                    ho_ref[...] = hn.astype(ho_ref.dtype)


def _ffn(x, h, w_up, conv_k, conv_b, w_down, layer, g_next, final_norm):
    m = x.shape[0]
    tm = FFN_TM
    nb = tm // HALO
    last = m // HALO - 1
    row = lambda i, j: (i, 0)
    up_tile = lambda j: jnp.minimum(j, N_FF_TILES - 1)
    gate = lambda i, j: (layer, 0, up_tile(j))
    val = lambda i, j: (layer, 0, N_FF_TILES + up_tile(j))
    out_specs = [pl.BlockSpec((tm, D_MODEL), row)]
    out_shape = [jax.ShapeDtypeStruct((m, D_MODEL), _F32)]
    if not final_norm:
        out_specs.append(pl.BlockSpec((tm, D_MODEL), row))
        out_shape.append(jax.ShapeDtypeStruct((m, D_MODEL), _BF16))
    return pl.pallas_call(
        partial(_ffn_kernel, final_norm=final_norm),
        grid=(m // tm, N_FF_TILES + FFN_LAG),
        in_specs=[
            pl.BlockSpec(memory_space=pl.ANY),
            pl.BlockSpec((tm, D_MODEL), row),
            pl.BlockSpec((HALO, D_MODEL), lambda i, j: (jnp.maximum(i * nb - 1, 0), 0)),
            pl.BlockSpec((HALO, D_MODEL), lambda i, j: (jnp.minimum((i + 1) * nb, last), 0)),
            pl.BlockSpec((D_MODEL, FF_TILE), lambda i, j: (0, up_tile(j))),
            pl.BlockSpec((D_MODEL, FF_TILE), lambda i, j: (0, N_FF_TILES + up_tile(j))),
            pl.BlockSpec((None, 3, FF_TILE), gate),
            pl.BlockSpec((None, 3, FF_TILE), val),
            pl.BlockSpec((None, 1, FF_TILE), gate),
            pl.BlockSpec((None, 1, FF_TILE), val),
            pl.BlockSpec((FF_TILE, D_MODEL), lambda i, j: (jnp.clip(j - FFN_LAG, 0, N_FF_TILES - 1), 0)),
            pl.BlockSpec((1, D_MODEL), lambda i, j: (0, 0)),
        ],
        out_specs=out_specs,
        out_shape=out_shape,
        scratch_shapes=[
            pltpu.VMEM((tm + 2 * HALO, D_MODEL), _BF16),
            pltpu.VMEM((FF_TILE // LANES, tm + 2 * HALO, LANES), _F32),
            pltpu.VMEM((FF_TILE // LANES, tm + 2 * HALO, LANES), _F32),
            pltpu.VMEM((FFN_LAG + 1, tm, FF_TILE), _BF16),
            pltpu.SemaphoreType.DMA(()),
        ],
        compiler_params=_params(("arbitrary", "arbitrary")),
        name="conv_ffn",
    )(x, h, h, h, w_up, w_up, conv_k, conv_k, conv_b, conv_b, w_down, g_next)


def _pad_ff(a):
    gate, val = jnp.split(a, 2, axis=-1)
    pad = [(0, 0)] * (a.ndim - 1) + [(0, D_FF_PAD - D_FF)]
    return jnp.concatenate([jnp.pad(gate, pad), jnp.pad(val, pad)], axis=-1)


def kernel(x, norm_mix, w_in, gm_ln_g, gm_ln_b, gm_w_s, gm_b_s, na_rpb, w_branch, w_out,
           norm_ffn, w_up, conv_k, conv_b, w_down, norm_final):
    b, t, d = x.shape
    assert (b, t, d) == (1, SEQ, D_MODEL)
    xs = x.reshape(t, d)
    bias, h = _prologue(na_rpb, xs, norm_mix[0].reshape(1, d))
    conv_k_p, conv_b_p = _pad_ff(conv_k), _pad_ff(conv_b.reshape(DEPTH, 1, -1))
    w_s_b = gm_w_s.astype(_BF16)
    uv0, qkv0, gt0 = 0, 2 * GM_WIDTH, 2 * GM_WIDTH + 3 * NA_WIDTH
    n_uv = (2 * GM_WIDTH // PROJ_TN) * (t // PROJ_TM)
    n_gt = (2 * D_MODEL // PROJ_TN) * (t // PROJ_TM)
    uv_sides = (
        SideCast(_cast_down_rows, w_down, D_FF_PAD // n_uv, D_FF_PAD, d),
        SideCast(_cast_rows, w_branch, w_branch.shape[1] // n_uv, w_branch.shape[1], d),
        SideCast(_cast_rows, w_out, d // n_uv, d, d),
    )
    gt_sides = (SideCast(_cast_up_rows, w_up, d // n_gt, d, 2 * D_FF_PAD),)
    for l in range(DEPTH):
        uv, w_down_b, w_branch_b, w_out_b = _proj(h, w_in, l, uv0, 2 * GM_WIDTH, "gelu",
                                                  "proj_uv", uv_sides)
        qkv, = _proj(h, w_in, l, qkv0, 3 * NA_WIDTH, "none", "proj_qkv")
        gates, w_up_b = _proj(h, w_in, l, gt0, 2 * D_MODEL, "sigmoid", "proj_gates", gt_sides)
        yb = _na(qkv, bias, l)
        xs, h = _merge(xs, uv, gm_ln_g[l].reshape(1, -1), gm_ln_b[l].reshape(1, -1), w_s_b[l],
                       gm_b_s[l].reshape(GM_GROUPS, CHUNK, 1), yb, gates, w_branch_b, w_out_b,
                       norm_ffn[l].reshape(1, d))
        if l == DEPTH - 1:
            xs, = _ffn(xs, h, w_up_b, conv_k_p, conv_b_p, w_down_b, l,
                       norm_final.reshape(1, d), True)
        else:
            xs, h = _ffn(xs, h, w_up_b, conv_k_p, conv_b_p, w_down_b, l,
                         norm_mix[l + 1].reshape(1, d), False)
    return xs.reshape(b, t, d)
```

```python
from functools import partial
from typing import Callable, NamedTuple

import jax
import jax.numpy as jnp
from jax import lax
from jax.experimental import pallas as pl
from jax.experimental.pallas import tpu as pltpu

D_MODEL = 2048
SEQ = 8192
DEPTH = 2
GRID_W = 64
ROWS = SEQ // GRID_W
GM_GROUPS = 8
GM_HEAD = 128
GM_WIDTH = GM_GROUPS * GM_HEAD
CHUNK = 128
NA_HEADS = 16
NA_HEAD_DIM = 64
NA_WIDTH = NA_HEADS * NA_HEAD_DIM
NA_KR = 8
NA_KC = 16
D_FF = 5504
RMS_EPS = 1e-6
LN_EPS = 1e-5
NEG_INF = -1e30

LANES = 128
BF16_SUBLANES = 16
VMEM_BYTES = 64 * 1024 * 1024
VMEM_LIMIT_BYTES = VMEM_BYTES - 4 * 1024 * 1024

FF_TILE = 512
D_FF_PAD = ((D_FF + FF_TILE - 1) // FF_TILE) * FF_TILE
N_FF_TILES = D_FF_PAD // FF_TILE
PROJ_TM = 1024
PROJ_TN = 1024
MERGE_TM = 512
FFN_TM = 1024
FFN_LAG = 1
HALO = BF16_SUBLANES
NA_KTOK = NA_KR * GRID_W
NA_ROWS_PER_STEP = 32
NA_UNROLL = 8
NA_CLASS_ROW = (0, 1, 2, 3, ROWS // 2, ROWS - 3, ROWS - 2, ROWS - 1)
N_CLASSES = len(NA_CLASS_ROW)
NA_INTERIOR_CLASS = NA_KR // 2

_BF16 = jnp.bfloat16
_F32 = jnp.float32


def _params(sem):
    return pltpu.CompilerParams(dimension_semantics=sem, vmem_limit_bytes=VMEM_LIMIT_BYTES)


def _rms(x, g):
    return x * lax.rsqrt(jnp.mean(x * x, axis=-1, keepdims=True) + RMS_EPS) * g


def _gelu_tanh(x):
    return 0.5 * x * (1.0 + jnp.tanh(0.7978845608028654 * (x + 0.044715 * (x * x * x))))


def _sigmoid(x):
    return 1.0 / (1.0 + jnp.exp(-x))


_ACTS = {"gelu": _gelu_tanh, "sigmoid": _sigmoid, "none": lambda x: x}


class SideCast(NamedTuple):
    body: Callable
    array: jax.Array
    rows: int
    out_rows: int
    out_cols: int


def _cast_rows(x_ref, o_ref, step):
    del step
    o_ref[...] = x_ref[...].astype(o_ref.dtype)


def _cast_up_rows(x_ref, o_ref, step):
    del step
    zeros = jnp.zeros((x_ref.shape[0], D_FF_PAD - D_FF), o_ref.dtype)
    o_ref[:, :D_FF] = x_ref[:, :D_FF].astype(o_ref.dtype)
    o_ref[:, D_FF:D_FF_PAD] = zeros
    o_ref[:, D_FF_PAD:D_FF_PAD + D_FF] = x_ref[:, D_FF:].astype(o_ref.dtype)
    o_ref[:, D_FF_PAD + D_FF:] = zeros


def _cast_down_rows(x_ref, o_ref, step):
    row = step * x_ref.shape[0] + lax.broadcasted_iota(jnp.int32, x_ref.shape, 0)
    o_ref[...] = jnp.where(row < D_FF, x_ref[...], 0.0).astype(o_ref.dtype)


def _proj_kernel(*refs, act, side_bodies):
    n = len(side_bodies)
    h_ref, w_ref = refs[:2]
    side_in = refs[2:2 + n]
    o_ref = refs[2 + n]
    side_out = refs[3 + n:3 + 2 * n]
    w_scr = refs[3 + 2 * n]

    @pl.when(pl.program_id(1) == 0)
    def _():
        w_scr[...] = w_ref[...].astype(_BF16)

    step = pl.program_id(0) * pl.num_programs(1) + pl.program_id(1)
    for body, x_ref, y_ref in zip(side_bodies, side_in, side_out):
        body(x_ref, y_ref, step)

    acc = jnp.dot(h_ref[...], w_scr[...], preferred_element_type=_F32)
    o_ref[...] = _ACTS[act](acc).astype(o_ref.dtype)


def _proj(h, w_in, layer, col0, ncols, act, name, sides=()):
    m = h.shape[0]
    jb0 = col0 // PROJ_TN
    n_i = m // PROJ_TM
    grid = (ncols // PROJ_TN, n_i)
    slab = lambda j, i: (j * n_i + i, 0)
    for s in sides:
        assert s.rows * grid[0] * grid[1] == s.out_rows, (name, s.rows, s.out_rows)
    outs = pl.pallas_call(
        partial(_proj_kernel, act=act, side_bodies=tuple(s.body for s in sides)),
        grid=grid,
        in_specs=[
            pl.BlockSpec((PROJ_TM, D_MODEL), lambda j, i: (i, 0)),
            pl.BlockSpec((None, D_MODEL, PROJ_TN), lambda j, i: (layer, 0, jb0 + j)),
        ] + [pl.BlockSpec((None, s.rows, s.array.shape[2]), lambda j, i: (layer, j * n_i + i, 0))
             for s in sides],
        out_specs=[pl.BlockSpec((PROJ_TM, PROJ_TN), lambda j, i: (i, j))]
        + [pl.BlockSpec((s.rows, s.out_cols), slab) for s in sides],
        out_shape=[jax.ShapeDtypeStruct((m, ncols), _BF16)]
        + [jax.ShapeDtypeStruct((s.out_rows, s.out_cols), _BF16) for s in sides],
        scratch_shapes=[pltpu.VMEM((D_MODEL, PROJ_TN), _BF16)],
        compiler_params=_params(("arbitrary", "arbitrary")),
        name=name,
    )(h, w_in, *[s.array for s in sides])
    return outs


def _gmlp_gate(uv_ref, lng_ref, lnb_ref, ws_ref, bs_ref, ya_ref):
    v = uv_ref[:, GM_WIDTH:].astype(_F32)
    mu = jnp.mean(v, axis=-1, keepdims=True)
    vc = v - mu
    var = jnp.mean(vc * vc, axis=-1, keepdims=True)
    vn = (vc * lax.rsqrt(var + LN_EPS) * lng_ref[...] + lnb_ref[...]).astype(_BF16)
    for n in range(uv_ref.shape[0] // CHUNK):
        rows = slice(n * CHUNK, (n + 1) * CHUNK)
        for g in range(GM_GROUPS):
            cols = slice(g * GM_HEAD, (g + 1) * GM_HEAD)
            mixed = jnp.dot(ws_ref[g], vn[rows, cols], preferred_element_type=_F32) + bs_ref[g]
            ya_ref[rows, cols] = (uv_ref[rows, cols].astype(_F32) * mixed).astype(ya_ref.dtype)


RPB_LANE0 = LANES // 2 - (NA_KC - 1)


def _prologue_kernel(rpb_ref, x_ref, g_ref, b_ref, h_ref):
    h_ref[...] = _rms(x_ref[...], g_ref[...]).astype(h_ref.dtype)

    c = lax.broadcasted_iota(jnp.int32, (GRID_W, LANES), 0)
    lane = lax.broadcasted_iota(jnp.int32, (GRID_W, LANES), 1)
    first = lane < GRID_W
    k = jnp.where(first, lane, lane - GRID_W)
    cs = jnp.clip(c - NA_KC // 2, 0, GRID_W - NA_KC)
    in_win = (k >= cs) & (k < cs + NA_KC)

    def toeplitz(dr, shift):
        row = jnp.broadcast_to(rpb_ref[0, dr:dr + 1, :], (GRID_W, LANES))
        return pltpu.roll(row, shift, axis=1, stride=1, stride_axis=0)

    n_dr = 2 * NA_KR - 1
    pairs = [jnp.where(in_win, jnp.where(first, toeplitz(dr, LANES // 2), toeplitz(dr + 1, 0)),
                       NEG_INF) for dr in range(n_dr - 1)]
    for cls, r in enumerate(NA_CLASS_ROW):
        rs = min(max(r - NA_KR // 2, 0), ROWS - NA_KR)
        for i in range(0, NA_KR, 2):
            b_ref[0, cls, :, i * GRID_W:(i + 2) * GRID_W] = pairs[rs + i - r + NA_KR - 1]


def _prologue(rpb, x, g):
    n = DEPTH * NA_HEADS
    m = x.shape[0]
    rows = m // n
    rpb_p = jnp.pad(rpb.reshape(n, 2 * NA_KR - 1, 2 * NA_KC - 1),
                    ((0, 0), (0, 1), (RPB_LANE0, LANES - RPB_LANE0 - (2 * NA_KC - 1))))
    return pl.pallas_call(
        _prologue_kernel,
        grid=(n,),
        in_specs=[pl.BlockSpec((1, 2 * NA_KR, LANES), lambda s: (s, 0, 0)),
                  pl.BlockSpec((rows, D_MODEL), lambda s: (s, 0)),
                  pl.BlockSpec((1, D_MODEL), lambda s: (0, 0))],
        out_specs=[pl.BlockSpec((1, N_CLASSES, GRID_W, NA_KTOK), lambda s: (s // 2, 0, s % 2, 0)),
                   pl.BlockSpec((rows, D_MODEL), lambda s: (s, 0))],
        out_shape=[jax.ShapeDtypeStruct((n // 2, N_CLASSES, 2 * GRID_W, NA_KTOK), _F32),
                   jax.ShapeDtypeStruct((m, D_MODEL), _BF16)],
        compiler_params=_params(("arbitrary",)),
        name="prologue",
    )(rpb_p, x, g)


def _na_window(r):
    rs = jnp.clip(r - NA_KR // 2, 0, ROWS - NA_KR)
    cls = jnp.where(r < NA_INTERIOR_CLASS, r,
                    jnp.where(r > ROWS - NA_KR // 2, r - (ROWS - NA_KR), NA_INTERIOR_CLASS))
    return pl.ds(pl.multiple_of(rs * GRID_W, GRID_W), NA_KTOK), cls


def _na_kernel(q_ref, k_ref, v_ref, b_ref, o_ref, p_scr, inv_scr):
    rb = pl.program_id(1)
    n_rb = pl.num_programs(1) - 1
    scale = NA_HEAD_DIM ** -0.5
    head0 = lax.broadcasted_iota(jnp.int32, (GRID_W, LANES), 1) < NA_HEAD_DIM

    def run(scores, apply):
        def group(g, carry):
            for u in range(NA_UNROLL):
                t = g * NA_UNROLL + u
                qrows = pl.ds(pl.multiple_of(t * GRID_W, GRID_W), GRID_W)
                if scores:
                    krows, cls = _na_window(rb * NA_ROWS_PER_STEP + t)
                    q = q_ref[qrows, :] * scale
                    zero = jnp.zeros_like(q)
                    q2 = jnp.concatenate([jnp.where(head0, q, zero), jnp.where(head0, zero, q)],
                                         axis=0)
                    s = lax.dot_general(q2, k_ref[krows, :], (((1,), (1,)), ((), ())),
                                        preferred_element_type=_F32) + b_ref[0, cls]
                    p = jnp.exp(s - jnp.max(s, axis=-1, keepdims=True))
                    den = jnp.sum(p, axis=-1, keepdims=True)
                if apply:
                    vrows, _ = _na_window((rb - 1) * NA_ROWS_PER_STEP + t)
                    o = jnp.dot(p_scr[t], v_ref[vrows, :],
                                preferred_element_type=_F32) * inv_scr[t]
                    o_ref[qrows, :] = jnp.where(head0, o[:GRID_W], o[GRID_W:]).astype(o_ref.dtype)
                if scores:
                    p_scr[t] = p.astype(_BF16)
                    inv_scr[t] = jnp.broadcast_to(1.0 / den, (2 * GRID_W, LANES))
            return carry

        lax.fori_loop(0, NA_ROWS_PER_STEP // NA_UNROLL, group, 0)

    pl.when(rb == 0)(lambda: run(True, False))
    pl.when((rb > 0) & (rb < n_rb))(lambda: run(True, True))
    pl.when(rb == n_rb)(lambda: run(False, True))


def _na(qkv, bias, layer):
    m = qkv.shape[0]
    tq = NA_ROWS_PER_STEP * GRID_W
    n_hp = NA_WIDTH // LANES
    n_rb = m // tq
    return pl.pallas_call(
        _na_kernel,
        grid=(n_hp, n_rb + 1),
        in_specs=[
            pl.BlockSpec((tq, LANES), lambda hp, rb: (jnp.minimum(rb, n_rb - 1), hp)),
            pl.BlockSpec((m, LANES), lambda hp, rb: (0, n_hp + hp)),
            pl.BlockSpec((m, LANES), lambda hp, rb: (0, 2 * n_hp + hp)),
            pl.BlockSpec((1, N_CLASSES, 2 * GRID_W, NA_KTOK),
                         lambda hp, rb: (layer * n_hp + hp, 0, 0, 0)),
        ],
        out_specs=pl.BlockSpec((tq, LANES), lambda hp, rb: (jnp.maximum(rb - 1, 0), hp)),
        out_shape=jax.ShapeDtypeStruct((m, NA_WIDTH), _BF16),
        scratch_shapes=[
            pltpu.VMEM((NA_ROWS_PER_STEP, 2 * GRID_W, NA_KTOK), _BF16),
            pltpu.VMEM((NA_ROWS_PER_STEP, 2 * GRID_W, LANES), _F32),
        ],
        compiler_params=_params(("arbitrary", "arbitrary")),
        name="na_attn",
    )(qkv, qkv, qkv, bias)


def _merge_kernel(x_ref, uv_ref, lng_ref, lnb_ref, ws_ref, bs_ref, yb_ref, gt_ref, wb_ref, wo_ref,
                  g_ref, o_ref, h_ref, ya_scr):
    _gmlp_gate(uv_ref, lng_ref, lnb_ref, ws_ref, bs_ref, ya_scr)
    ta = jnp.dot(ya_scr[...], wb_ref[:GM_WIDTH, :], preferred_element_type=_F32)
    tb = jnp.dot(yb_ref[...], wb_ref[GM_WIDTH:, :], preferred_element_type=_F32)
    merged = (gt_ref[:, :D_MODEL].astype(_F32) * ta + gt_ref[:, D_MODEL:].astype(_F32) * tb)
    y = x_ref[...] + jnp.dot(merged.astype(_BF16), wo_ref[...], preferred_element_type=_F32)
    o_ref[...] = y
    h_ref[...] = _rms(y, g_ref[...]).astype(h_ref.dtype)


def _merge(x, uv, ln_g, ln_b, w_s, b_s, yb, gates, w_branch, w_out, g_ffn):
    m = x.shape[0]
    resident = dict(pipeline_mode=pl.Buffered(1))
    row = lambda i: (i, 0)
    fixed2 = lambda i: (0, 0)
    fixed3 = lambda i: (0, 0, 0)
    return pl.pallas_call(
        _merge_kernel,
        grid=(m // MERGE_TM,),
        in_specs=[
            pl.BlockSpec((MERGE_TM, D_MODEL), row),
            pl.BlockSpec((MERGE_TM, 2 * GM_WIDTH), row),
            pl.BlockSpec((1, GM_WIDTH), fixed2),
            pl.BlockSpec((1, GM_WIDTH), fixed2),
            pl.BlockSpec((GM_GROUPS, CHUNK, CHUNK), fixed3),
            pl.BlockSpec((GM_GROUPS, CHUNK, 1), fixed3),
            pl.BlockSpec((MERGE_TM, NA_WIDTH), row),
            pl.BlockSpec((MERGE_TM, 2 * D_MODEL), row),
            pl.BlockSpec((GM_WIDTH + NA_WIDTH, D_MODEL), fixed2, **resident),
            pl.BlockSpec((D_MODEL, D_MODEL), fixed2, **resident),
            pl.BlockSpec((1, D_MODEL), fixed2),
        ],
        out_specs=[pl.BlockSpec((MERGE_TM, D_MODEL), row), pl.BlockSpec((MERGE_TM, D_MODEL), row)],
        out_shape=[jax.ShapeDtypeStruct((m, D_MODEL), _F32),
                   jax.ShapeDtypeStruct((m, D_MODEL), _BF16)],
        scratch_shapes=[pltpu.VMEM((MERGE_TM, GM_WIDTH), _BF16)],
        compiler_params=_params(("arbitrary",)),
        name="merge_out",
    )(x, uv, ln_g, ln_b, w_s, b_s, yb, gates, w_branch, w_out, g_ffn)


def _ffn_kernel(x_hbm, h_ref, hp_ref, hn_ref, wg_ref, wv_ref, kg_ref, kv_ref, bg_ref, bv_ref,
                wd_ref, gn_ref, *refs, final_norm):
    if final_norm:
        o_ref, h_scr, ug_scr, uv_scr, act_scr, x_sem = refs
    else:
        o_ref, ho_ref, h_scr, ug_scr, uv_scr, act_scr, x_sem = refs
    i, j = pl.program_id(0), pl.program_id(1)
    tm = FFN_TM

    def x_copy():
        rows = pl.ds(pl.multiple_of(i * tm, tm), tm)
        return pltpu.make_async_copy(x_hbm.at[rows, :], o_ref, x_sem)

    def up_dots():
        h = h_scr[...]
        for u_scr, w_ref in ((ug_scr, wg_ref), (uv_scr, wv_ref)):
            u = jnp.dot(h, w_ref[...], preferred_element_type=_F32)
            for c in range(FF_TILE // LANES):
                u_scr[c] = u[:, c * LANES:(c + 1) * LANES]

    def conv(u_scr, k_ref, b_ref, c):
        cols = slice(c * LANES, (c + 1) * LANES)
        return (u_scr[c, HALO - 1:HALO - 1 + tm, :] * k_ref[0:1, cols]
                + u_scr[c, HALO:HALO + tm, :] * k_ref[1:2, cols]
                + u_scr[c, HALO + 1:HALO + 1 + tm, :] * k_ref[2:3, cols] + b_ref[:, cols])

    def gate_into(slot):
        for c in range(FF_TILE // LANES):
            gate = conv(ug_scr, kg_ref, bg_ref, c)
            val = conv(uv_scr, kv_ref, bv_ref, c)
            act_scr[slot, :, c * LANES:(c + 1) * LANES] = (
                gate * _sigmoid(gate) * val).astype(_BF16)

    def down_from(slot):
        o_ref[...] += jnp.dot(act_scr[slot], wd_ref[...], preferred_element_type=_F32)

    @pl.when(j == 0)
    def _():
        zero = jnp.zeros((HALO, D_MODEL), _BF16)
        h_scr[:HALO, :] = jnp.where(i > 0, hp_ref[...], zero)
        h_scr[HALO:HALO + tm, :] = h_ref[...]
        h_scr[HALO + tm:, :] = jnp.where(i < pl.num_programs(0) - 1, hn_ref[...], zero)

    slots = FFN_LAG + 1
    for s in range(FFN_LAG):
        @pl.when(j == s)
        def _(s=s):
            if s == FFN_LAG - 1:
                x_copy().start()
            up_dots()
            gate_into(s)

    @pl.when(j == FFN_LAG)
    def _():
        x_copy().wait()

    for r in range(slots):
        @pl.when((j >= FFN_LAG) & (j < N_FF_TILES) & (j % slots == r))
        def _(r=r):
            up_dots()
            down_from((r - FFN_LAG) % slots)
            gate_into(r)

    for s in range(N_FF_TILES, N_FF_TILES + FFN_LAG):
        @pl.when(j == s)
        def _(s=s):
            down_from((s - FFN_LAG) % slots)
            if s == N_FF_TILES + FFN_LAG - 1:
                hn = _rms(o_ref[...], gn_ref[...])
                if final_norm:
                    o_ref[...] = hn
                else:
                    ho_ref[...] = hn.astype(ho_ref.dtype)


def _ffn(x, h, w_up, conv_k, conv_b, w_down, layer, g_next, final_norm):
    m = x.shape[0]
    tm = FFN_TM
    nb = tm // HALO
    last = m // HALO - 1
    row = lambda i, j: (i, 0)
    up_tile = lambda j: jnp.minimum(j, N_FF_TILES - 1)
    gate = lambda i, j: (layer, 0, up_tile(j))
    val = lambda i, j: (layer, 0, N_FF_TILES + up_tile(j))
    out_specs = [pl.BlockSpec((tm, D_MODEL), row)]
    out_shape = [jax.ShapeDtypeStruct((m, D_MODEL), _F32)]
    if not final_norm:
        out_specs.append(pl.BlockSpec((tm, D_MODEL), row))
        out_shape.append(jax.ShapeDtypeStruct((m, D_MODEL), _BF16))
    return pl.pallas_call(
        partial(_ffn_kernel, final_norm=final_norm),
        grid=(m // tm, N_FF_TILES + FFN_LAG),
        in_specs=[
            pl.BlockSpec(memory_space=pl.ANY),
            pl.BlockSpec((tm, D_MODEL), row),
            pl.BlockSpec((HALO, D_MODEL), lambda i, j: (jnp.maximum(i * nb - 1, 0), 0)),
            pl.BlockSpec((HALO, D_MODEL), lambda i, j: (jnp.minimum((i + 1) * nb, last), 0)),
            pl.BlockSpec((D_MODEL, FF_TILE), lambda i, j: (0, up_tile(j))),
            pl.BlockSpec((D_MODEL, FF_TILE), lambda i, j: (0, N_FF_TILES + up_tile(j))),
            pl.BlockSpec((None, 3, FF_TILE), gate),
            pl.BlockSpec((None, 3, FF_TILE), val),
            pl.BlockSpec((None, 1, FF_TILE), gate),
            pl.BlockSpec((None, 1, FF_TILE), val),
            pl.BlockSpec((FF_TILE, D_MODEL),
                         lambda i, j: (jnp.clip(j - FFN_LAG, 0, N_FF_TILES - 1), 0)),
            pl.BlockSpec((1, D_MODEL), lambda i, j: (0, 0)),
        ],
        out_specs=out_specs,
        out_shape=out_shape,
        scratch_shapes=[
            pltpu.VMEM((tm + 2 * HALO, D_MODEL), _BF16),
            pltpu.VMEM((FF_TILE // LANES, tm + 2 * HALO, LANES), _F32),
            pltpu.VMEM((FF_TILE // LANES, tm + 2 * HALO, LANES), _F32),
            pltpu.VMEM((FFN_LAG + 1, tm, FF_TILE), _BF16),
            pltpu.SemaphoreType.DMA(()),
        ],
        compiler_params=_params(("arbitrary", "arbitrary")),
        name="conv_ffn",
    )(x, h, h, h, w_up, w_up, conv_k, conv_k, conv_b, conv_b, w_down, g_next)


def _pad_ff(a):
    gate, val = jnp.split(a, 2, axis=-1)
    pad = [(0, 0)] * (a.ndim - 1) + [(0, D_FF_PAD - D_FF)]
    return jnp.concatenate([jnp.pad(gate, pad), jnp.pad(val, pad)], axis=-1)


def kernel(x, norm_mix, w_in, gm_ln_g, gm_ln_b, gm_w_s, gm_b_s, na_rpb, w_branch, w_out,
           norm_ffn, w_up, conv_k, conv_b, w_down, norm_final):
    b, t, d = x.shape
    assert (b, t, d) == (1, SEQ, D_MODEL)
    xs = x.reshape(t, d)
    bias, h = _prologue(na_rpb, xs, norm_mix[0].reshape(1, d))
    conv_k_p, conv_b_p = _pad_ff(conv_k), _pad_ff(conv_b.reshape(DEPTH, 1, -1))
    w_s_b = gm_w_s.astype(_BF16)
    uv0, qkv0, gt0 = 0, 2 * GM_WIDTH, 2 * GM_WIDTH + 3 * NA_WIDTH
    n_uv = (2 * GM_WIDTH // PROJ_TN) * (t // PROJ_TM)
    n_gt = (2 * D_MODEL // PROJ_TN) * (t // PROJ_TM)
    uv_sides = (
        SideCast(_cast_down_rows, w_down, D_FF_PAD // n_uv, D_FF_PAD, d),
        SideCast(_cast_rows, w_branch, w_branch.shape[1] // n_uv, w_branch.shape[1], d),
        SideCast(_cast_rows, w_out, d // n_uv, d, d),
    )
    gt_sides = (SideCast(_cast_up_rows, w_up, d // n_gt, d, 2 * D_FF_PAD),)
    for l in range(DEPTH):
        uv, w_down_b, w_branch_b, w_out_b = _proj(h, w_in, l, uv0, 2 * GM_WIDTH, "gelu",
                                                  "proj_uv", uv_sides)
        qkv, = _proj(h, w_in, l, qkv0, 3 * NA_WIDTH, "none", "proj_qkv")
        gates, w_up_b = _proj(h, w_in, l, gt0, 2 * D_MODEL, "sigmoid", "proj_gates", gt_sides)
        yb = _na(qkv, bias, l)
        xs, h = _merge(xs, uv, gm_ln_g[l].reshape(1, -1), gm_ln_b[l].reshape(1, -1), w_s_b[l],
                       gm_b_s[l].reshape(GM_GROUPS, CHUNK, 1), yb, gates, w_branch_b, w_out_b,
                       norm_ffn[l].reshape(1, d))
        if l == DEPTH - 1:
            xs, = _ffn(xs, h, w_up_b, conv_k_p, conv_b_p, w_down_b, l,
                       norm_final.reshape(1, d), True)
        else:
            xs, h = _ffn(xs, h, w_up_b, conv_k_p, conv_b_p, w_down_b, l,
                         norm_mix[l + 1].reshape(1, d), False)
    return xs.reshape(b, t, d)
```

```python
from functools import partial
from typing import Callable, NamedTuple

import jax
import jax.numpy as jnp
from jax import lax
from jax.experimental import pallas as pl
from jax.experimental.pallas import tpu as pltpu

D_MODEL = 2048
SEQ = 8192
DEPTH = 2
GRID_W = 64
ROWS = SEQ // GRID_W
GM_GROUPS = 8
GM_HEAD = 128
GM_WIDTH = GM_GROUPS * GM_HEAD
CHUNK = 128
NA_HEADS = 16
NA_HEAD_DIM = 64
NA_WIDTH = NA_HEADS * NA_HEAD_DIM
NA_KR = 8
NA_KC = 16
D_FF = 5504
RMS_EPS = 1e-6
LN_EPS = 1e-5
NEG_INF = -1e30

LANES = 128
BF16_SUBLANES = 16
VMEM_BYTES = 64 * 1024 * 1024
VMEM_LIMIT_BYTES = VMEM_BYTES - 4 * 1024 * 1024

FF_TILE = 512
D_FF_PAD = ((D_FF + FF_TILE - 1) // FF_TILE) * FF_TILE
N_FF_TILES = D_FF_PAD // FF_TILE
PROJ_TM = 1024
PROJ_TN = 1024
MERGE_TM = 512
FFN_TM = 1024
FFN_LAG = 1
HALO = BF16_SUBLANES
NA_KTOK = NA_KR * GRID_W
NA_ROWS_PER_STEP = 32
NA_UNROLL = 32
NA_CLASS_ROW = (0, 1, 2, 3, ROWS // 2, ROWS - 3, ROWS - 2, ROWS - 1)
N_CLASSES = len(NA_CLASS_ROW)
NA_INTERIOR_CLASS = NA_KR // 2

_BF16 = jnp.bfloat16
_F32 = jnp.float32


def _params(sem):
    return pltpu.CompilerParams(dimension_semantics=sem, vmem_limit_bytes=VMEM_LIMIT_BYTES)


def _rms(x, g):
    return x * lax.rsqrt(jnp.mean(x * x, axis=-1, keepdims=True) + RMS_EPS) * g


def _gelu_tanh(x):
    return 0.5 * x * (1.0 + jnp.tanh(0.7978845608028654 * (x + 0.044715 * (x * x * x))))


def _sigmoid(x):
    return 1.0 / (1.0 + jnp.exp(-x))


_ACTS = {"gelu": _gelu_tanh, "sigmoid": _sigmoid, "none": lambda x: x}


class SideCast(NamedTuple):
    body: Callable
    array: jax.Array
    rows: int
    out_rows: int
    out_cols: int


def _cast_rows(x_ref, o_ref, step):
    del step
    o_ref[...] = x_ref[...].astype(o_ref.dtype)


def _cast_up_rows(x_ref, o_ref, step):
    del step
    zeros = jnp.zeros((x_ref.shape[0], D_FF_PAD - D_FF), o_ref.dtype)
    o_ref[:, :D_FF] = x_ref[:, :D_FF].astype(o_ref.dtype)
    o_ref[:, D_FF:D_FF_PAD] = zeros
    o_ref[:, D_FF_PAD:D_FF_PAD + D_FF] = x_ref[:, D_FF:].astype(o_ref.dtype)
    o_ref[:, D_FF_PAD + D_FF:] = zeros


def _cast_down_rows(x_ref, o_ref, step):
    row = step * x_ref.shape[0] + lax.broadcasted_iota(jnp.int32, x_ref.shape, 0)
    o_ref[...] = jnp.where(row < D_FF, x_ref[...], 0.0).astype(o_ref.dtype)


def _proj_kernel(*refs, act, side_bodies):
    n = len(side_bodies)
    h_ref, w_ref = refs[:2]
    side_in = refs[2:2 + n]
    o_ref = refs[2 + n]
    side_out = refs[3 + n:3 + 2 * n]
    w_scr = refs[3 + 2 * n]

    @pl.when(pl.program_id(1) == 0)
    def _():
        w_scr[...] = w_ref[...].astype(_BF16)

    step = pl.program_id(0) * pl.num_programs(1) + pl.program_id(1)
    for body, x_ref, y_ref in zip(side_bodies, side_in, side_out):
        body(x_ref, y_ref, step)

    acc = jnp.dot(h_ref[...], w_scr[...], preferred_element_type=_F32)
    o_ref[...] = _ACTS[act](acc).astype(o_ref.dtype)


def _proj(h, w_in, layer, col0, ncols, act, name, sides=()):
    m = h.shape[0]
    jb0 = col0 // PROJ_TN
    n_i = m // PROJ_TM
    grid = (ncols // PROJ_TN, n_i)
    slab = lambda j, i: (j * n_i + i, 0)
    for s in sides:
        assert s.rows * grid[0] * grid[1] == s.out_rows, (name, s.rows, s.out_rows)
    outs = pl.pallas_call(
        partial(_proj_kernel, act=act, side_bodies=tuple(s.body for s in sides)),
        grid=grid,
        in_specs=[
            pl.BlockSpec((PROJ_TM, D_MODEL), lambda j, i: (i, 0)),
            pl.BlockSpec((None, D_MODEL, PROJ_TN), lambda j, i: (layer, 0, jb0 + j)),
        ] + [pl.BlockSpec((None, s.rows, s.array.shape[2]), lambda j, i: (layer, j * n_i + i, 0))
             for s in sides],
        out_specs=[pl.BlockSpec((PROJ_TM, PROJ_TN), lambda j, i: (i, j))]
        + [pl.BlockSpec((s.rows, s.out_cols), slab) for s in sides],
        out_shape=[jax.ShapeDtypeStruct((m, ncols), _BF16)]
        + [jax.ShapeDtypeStruct((s.out_rows, s.out_cols), _BF16) for s in sides],
        scratch_shapes=[pltpu.VMEM((D_MODEL, PROJ_TN), _BF16)],
        compiler_params=_params(("arbitrary", "arbitrary")),
        name=name,
    )(h, w_in, *[s.array for s in sides])
    return outs


def _gmlp_gate(uv_ref, lng_ref, lnb_ref, ws_ref, bs_ref, ya_ref):
    v = uv_ref[:, GM_WIDTH:].astype(_F32)
    mu = jnp.mean(v, axis=-1, keepdims=True)
    vc = v - mu
    var = jnp.mean(vc * vc, axis=-1, keepdims=True)
    vn = (vc * lax.rsqrt(var + LN_EPS) * lng_ref[...] + lnb_ref[...]).astype(_BF16)
    for n in range(uv_ref.shape[0] // CHUNK):
        rows = slice(n * CHUNK, (n + 1) * CHUNK)
        for g in range(GM_GROUPS):
            cols = slice(g * GM_HEAD, (g + 1) * GM_HEAD)
            mixed = jnp.dot(ws_ref[g], vn[rows, cols], preferred_element_type=_F32) + bs_ref[g]
            ya_ref[rows, cols] = (uv_ref[rows, cols].astype(_F32) * mixed).astype(ya_ref.dtype)


RPB_LANE0 = LANES // 2 - (NA_KC - 1)


def _prologue_kernel(rpb_ref, x_ref, g_ref, b_ref, h_ref):
    h_ref[...] = _rms(x_ref[...], g_ref[...]).astype(h_ref.dtype)

    c = lax.broadcasted_iota(jnp.int32, (GRID_W, LANES), 0)
    lane = lax.broadcasted_iota(jnp.int32, (GRID_W, LANES), 1)
    first = lane < GRID_W
    k = jnp.where(first, lane, lane - GRID_W)
    cs = jnp.clip(c - NA_KC // 2, 0, GRID_W - NA_KC)
    in_win = (k >= cs) & (k < cs + NA_KC)

    def toeplitz(dr, shift):
        row = jnp.broadcast_to(rpb_ref[0, dr:dr + 1, :], (GRID_W, LANES))
        return pltpu.roll(row, shift, axis=1, stride=1, stride_axis=0)

    n_dr = 2 * NA_KR - 1
    pairs = [jnp.where(in_win, jnp.where(first, toeplitz(dr, LANES // 2), toeplitz(dr + 1, 0)),
                       NEG_INF) for dr in range(n_dr - 1)]
    for cls, r in enumerate(NA_CLASS_ROW):
        rs = min(max(r - NA_KR // 2, 0), ROWS - NA_KR)
        for i in range(0, NA_KR, 2):
            b_ref[0, cls, :, i * GRID_W:(i + 2) * GRID_W] = pairs[rs + i - r + NA_KR - 1]


def _prologue(rpb, x, g):
    n = DEPTH * NA_HEADS
    m = x.shape[0]
    rows = m // n
    rpb_p = jnp.pad(rpb.reshape(n, 2 * NA_KR - 1, 2 * NA_KC - 1),
                    ((0, 0), (0, 1), (RPB_LANE0, LANES - RPB_LANE0 - (2 * NA_KC - 1))))
    return pl.pallas_call(
        _prologue_kernel,
        grid=(n,),
        in_specs=[pl.BlockSpec((1, 2 * NA_KR, LANES), lambda s: (s, 0, 0)),
                  pl.BlockSpec((rows, D_MODEL), lambda s: (s, 0)),
                  pl.BlockSpec((1, D_MODEL), lambda s: (0, 0))],
        out_specs=[pl.BlockSpec((1, N_CLASSES, GRID_W, NA_KTOK), lambda s: (s // 2, 0, s % 2, 0)),
                   pl.BlockSpec((rows, D_MODEL), lambda s: (s, 0))],
        out_shape=[jax.ShapeDtypeStruct((n // 2, N_CLASSES, 2 * GRID_W, NA_KTOK), _F32),
                   jax.ShapeDtypeStruct((m, D_MODEL), _BF16)],
        compiler_params=_params(("arbitrary",)),
        name="prologue",
    )(rpb_p, x, g)


def _na_window(r):
    rs = jnp.clip(r - NA_KR // 2, 0, ROWS - NA_KR)
    cls = jnp.where(r < NA_INTERIOR_CLASS, r,
                    jnp.where(r > ROWS - NA_KR // 2, r - (ROWS - NA_KR), NA_INTERIOR_CLASS))
    return pl.ds(pl.multiple_of(rs * GRID_W, GRID_W), NA_KTOK), cls


def _na_kernel(q_ref, k_ref, v_ref, b_ref, o_ref, p_scr, inv_scr):
    rb = pl.program_id(1)
    n_rb = pl.num_programs(1) - 1
    scale = NA_HEAD_DIM ** -0.5
    head0 = lax.broadcasted_iota(jnp.int32, (GRID_W, LANES), 1) < NA_HEAD_DIM

    def run(scores, apply):
        def group(g, carry):
            for u in range(NA_UNROLL):
                t = g * NA_UNROLL + u
                qrows = pl.ds(pl.multiple_of(t * GRID_W, GRID_W), GRID_W)
                if scores:
                    krows, cls = _na_window(rb * NA_ROWS_PER_STEP + t)
                    q = q_ref[qrows, :] * scale
                    zero = jnp.zeros_like(q)
                    q2 = jnp.concatenate([jnp.where(head0, q, zero), jnp.where(head0, zero, q)],
                                         axis=0)
                    s = lax.dot_general(q2, k_ref[krows, :], (((1,), (1,)), ((), ())),
                                        preferred_element_type=_F32) + b_ref[0, cls]
                    p = jnp.exp(s - jnp.max(s, axis=-1, keepdims=True))
                    den = jnp.sum(p, axis=-1, keepdims=True)
                if apply:
                    vrows, _ = _na_window((rb - 1) * NA_ROWS_PER_STEP + t)
                    o = jnp.dot(p_scr[t], v_ref[vrows, :],
                                preferred_element_type=_F32) * inv_scr[t]
                    o_ref[qrows, :] = jnp.where(head0, o[:GRID_W], o[GRID_W:]).astype(o_ref.dtype)
                if scores:
                    p_scr[t] = p.astype(_BF16)
                    inv_scr[t] = jnp.broadcast_to(1.0 / den, (2 * GRID_W, LANES))
            return carry

        lax.fori_loop(0, NA_ROWS_PER_STEP // NA_UNROLL, group, 0)

    pl.when(rb == 0)(lambda: run(True, False))
    pl.when((rb > 0) & (rb < n_rb))(lambda: run(True, True))
    pl.when(rb == n_rb)(lambda: run(False, True))


def _na(qkv, bias, layer):
    m = qkv.shape[0]
    tq = NA_ROWS_PER_STEP * GRID_W
    n_hp = NA_WIDTH // LANES
    n_rb = m // tq
    return pl.pallas_call(
        _na_kernel,
        grid=(n_hp, n_rb + 1),
        in_specs=[
            pl.BlockSpec((tq, LANES), lambda hp, rb: (jnp.minimum(rb, n_rb - 1), hp)),
            pl.BlockSpec((m, LANES), lambda hp, rb: (0, n_hp + hp)),
            pl.BlockSpec((m, LANES), lambda hp, rb: (0, 2 * n_hp + hp)),
            pl.BlockSpec((1, N_CLASSES, 2 * GRID_W, NA_KTOK),
                         lambda hp, rb: (layer * n_hp + hp, 0, 0, 0)),
        ],
        out_specs=pl.BlockSpec((tq, LANES), lambda hp, rb: (jnp.maximum(rb - 1, 0), hp)),
        out_shape=jax.ShapeDtypeStruct((m, NA_WIDTH), _BF16),
        scratch_shapes=[
            pltpu.VMEM((NA_ROWS_PER_STEP, 2 * GRID_W, NA_KTOK), _BF16),
            pltpu.VMEM((NA_ROWS_PER_STEP, 2 * GRID_W, LANES), _F32),
        ],
        compiler_params=_params(("arbitrary", "arbitrary")),
        name="na_attn",
    )(qkv, qkv, qkv, bias)


def _merge_kernel(x_ref, uv_ref, lng_ref, lnb_ref, ws_ref, bs_ref, yb_ref, gt_ref, wb_ref, wo_ref,
                  g_ref, o_ref, h_ref, ya_scr):
    _gmlp_gate(uv_ref, lng_ref, lnb_ref, ws_ref, bs_ref, ya_scr)
    ta = jnp.dot(ya_scr[...], wb_ref[:GM_WIDTH, :], preferred_element_type=_F32)
    tb = jnp.dot(yb_ref[...], wb_ref[GM_WIDTH:, :], preferred_element_type=_F32)
    merged = (gt_ref[:, :D_MODEL].astype(_F32) * ta + gt_ref[:, D_MODEL:].astype(_F32) * tb)
    y = x_ref[...] + jnp.dot(merged.astype(_BF16), wo_ref[...], preferred_element_type=_F32)
    o_ref[...] = y
    h_ref[...] = _rms(y, g_ref[...]).astype(h_ref.dtype)


def _merge(x, uv, ln_g, ln_b, w_s, b_s, yb, gates, w_branch, w_out, g_ffn):
    m = x.shape[0]
    resident = dict(pipeline_mode=pl.Buffered(1))
    row = lambda i: (i, 0)
    fixed2 = lambda i: (0, 0)
    fixed3 = lambda i: (0, 0, 0)
    return pl.pallas_call(
        _merge_kernel,
        grid=(m // MERGE_TM,),
        in_specs=[
            pl.BlockSpec((MERGE_TM, D_MODEL), row),
            pl.BlockSpec((MERGE_TM, 2 * GM_WIDTH), row),
            pl.BlockSpec((1, GM_WIDTH), fixed2),
            pl.BlockSpec((1, GM_WIDTH), fixed2),
            pl.BlockSpec((GM_GROUPS, CHUNK, CHUNK), fixed3),
            pl.BlockSpec((GM_GROUPS, CHUNK, 1), fixed3),
            pl.BlockSpec((MERGE_TM, NA_WIDTH), row),
            pl.BlockSpec((MERGE_TM, 2 * D_MODEL), row),
            pl.BlockSpec((GM_WIDTH + NA_WIDTH, D_MODEL), fixed2, **resident),
            pl.BlockSpec((D_MODEL, D_MODEL), fixed2, **resident),
            pl.BlockSpec((1, D_MODEL), fixed2),
        ],
        out_specs=[pl.BlockSpec((MERGE_TM, D_MODEL), row), pl.BlockSpec((MERGE_TM, D_MODEL), row)],
        out_shape=[jax.ShapeDtypeStruct((m, D_MODEL), _F32),
                   jax.ShapeDtypeStruct((m, D_MODEL), _BF16)],
        scratch_shapes=[pltpu.VMEM((MERGE_TM, GM_WIDTH), _BF16)],
        compiler_params=_params(("arbitrary",)),
        name="merge_out",
    )(x, uv, ln_g, ln_b, w_s, b_s, yb, gates, w_branch, w_out, g_ffn)


def _ffn_kernel(x_hbm, h_ref, hp_ref, hn_ref, wg_ref, wv_ref, kg_ref, kv_ref, bg_ref, bv_ref,
                wd_ref, gn_ref, *refs, final_norm):
    if final_norm:
        o_ref, h_scr, ug_scr, uv_scr, act_scr, x_sem = refs
    else:
        o_ref, ho_ref, h_scr, ug_scr, uv_scr, act_scr, x_sem = refs
    i, j = pl.program_id(0), pl.program_id(1)
    tm = FFN_TM

    def x_copy():
        rows = pl.ds(pl.multiple_of(i * tm, tm), tm)
        return pltpu.make_async_copy(x_hbm.at[rows, :], o_ref, x_sem)

    def up_dots():
        h = h_scr[...]
        for u_scr, w_ref in ((ug_scr, wg_ref), (uv_scr, wv_ref)):
            u = jnp.dot(h, w_ref[...], preferred_element_type=_F32)
            for c in range(FF_TILE // LANES):
                u_scr[c] = u[:, c * LANES:(c + 1) * LANES]

    def conv(u_scr, k_ref, b_ref, c):
        cols = slice(c * LANES, (c + 1) * LANES)
        return (u_scr[c, HALO - 1:HALO - 1 + tm, :] * k_ref[0:1, cols]
                + u_scr[c, HALO:HALO + tm, :] * k_ref[1:2, cols]
                + u_scr[c, HALO + 1:HALO + 1 + tm, :] * k_ref[2:3, cols] + b_ref[:, cols])

    def gate_into(slot):
        for c in range(FF_TILE // LANES):
            gate = conv(ug_scr, kg_ref, bg_ref, c)
            val = conv(uv_scr, kv_ref, bv_ref, c)
            act_scr[slot, :, c * LANES:(c + 1) * LANES] = (
                gate * _sigmoid(gate) * val).astype(_BF16)

    def down_from(slot):
        o_ref[...] += jnp.dot(act_scr[slot], wd_ref[...], preferred_element_type=_F32)

    @pl.when(j == 0)
    def _():
        zero = jnp.zeros((HALO, D_MODEL), _BF16)
        h_scr[:HALO, :] = jnp.where(i > 0, hp_ref[...], zero)
        h_scr[HALO:HALO + tm, :] = h_ref[...]
        h_scr[HALO + tm:, :] = jnp.where(i < pl.num_programs(0) - 1, hn_ref[...], zero)

    slots = FFN_LAG + 1
    for s in range(FFN_LAG):
        @pl.when(j == s)
        def _(s=s):
            if s == FFN_LAG - 1:
                x_copy().start()
            up_dots()
            gate_into(s)

    @pl.when(j == FFN_LAG)
    def _():
        x_copy().wait()

    for r in range(slots):
        @pl.when((j >= FFN_LAG) & (j < N_FF_TILES) & (j % slots == r))
        def _(r=r):
            up_dots()
            down_from((r - FFN_LAG) % slots)
            gate_into(r)

    for s in range(N_FF_TILES, N_FF_TILES + FFN_LAG):
        @pl.when(j == s)
        def _(s=s):
            down_from((s - FFN_LAG) % slots)
            if s == N_FF_TILES + FFN_LAG - 1:
                hn = _rms(o_ref[...], gn_ref[...])
                if final_norm:
                    o_ref[...] = hn
                else:
                    ho_ref[...] = hn.astype(ho_ref.dtype)


def _ffn(x, h, w_up, conv_k, conv_b, w_down, layer, g_next, final_norm):
    m = x.shape[0]
    tm = FFN_TM
    nb = tm // HALO
    last = m // HALO - 1
    row = lambda i, j: (i, 0)
    up_tile = lambda j: jnp.minimum(j, N_FF_TILES - 1)
    gate = lambda i, j: (layer, 0, up_tile(j))
    val = lambda i, j: (layer, 0, N_FF_TILES + up_tile(j))
    out_specs = [pl.BlockSpec((tm, D_MODEL), row)]
    out_shape = [jax.ShapeDtypeStruct((m, D_MODEL), _F32)]
    if not final_norm:
        out_specs.append(pl.BlockSpec((tm, D_MODEL), row))
        out_shape.append(jax.ShapeDtypeStruct((m, D_MODEL), _BF16))
    return pl.pallas_call(
        partial(_ffn_kernel, final_norm=final_norm),
        grid=(m // tm, N_FF_TILES + FFN_LAG),
        in_specs=[
            pl.BlockSpec(memory_space=pl.ANY),
            pl.BlockSpec((tm, D_MODEL), row),
            pl.BlockSpec((HALO, D_MODEL), lambda i, j: (jnp.maximum(i * nb - 1, 0), 0)),
            pl.BlockSpec((HALO, D_MODEL), lambda i, j: (jnp.minimum((i + 1) * nb, last), 0)),
            pl.BlockSpec((D_MODEL, FF_TILE), lambda i, j: (0, up_tile(j))),
            pl.BlockSpec((D_MODEL, FF_TILE), lambda i, j: (0, N_FF_TILES + up_tile(j))),
            pl.BlockSpec((None, 3, FF_TILE), gate),
            pl.BlockSpec((None, 3, FF_TILE), val),
            pl.BlockSpec((None, 1, FF_TILE), gate),
            pl.BlockSpec((None, 1, FF_TILE), val),
            pl.BlockSpec((FF_TILE, D_MODEL),
                         lambda i, j: (jnp.clip(j - FFN_LAG, 0, N_FF_TILES - 1), 0)),
            pl.BlockSpec((1, D_MODEL), lambda i, j: (0, 0)),
        ],
        out_specs=out_specs,
        out_shape=out_shape,
        scratch_shapes=[
            pltpu.VMEM((tm + 2 * HALO, D_MODEL), _BF16),
            pltpu.VMEM((FF_TILE // LANES, tm + 2 * HALO, LANES), _F32),
            pltpu.VMEM((FF_TILE // LANES, tm + 2 * HALO, LANES), _F32),
            pltpu.VMEM((FFN_LAG + 1, tm, FF_TILE), _BF16),
            pltpu.SemaphoreType.DMA(()),
        ],
        compiler_params=_params(("arbitrary", "arbitrary")),
        name="conv_ffn",
    )(x, h, h, h, w_up, w_up, conv_k, conv_k, conv_b, conv_b, w_down, g_next)


def _pad_ff(a):
    gate, val = jnp.split(a, 2, axis=-1)
    pad = [(0, 0)] * (a.ndim - 1) + [(0, D_FF_PAD - D_FF)]
    return jnp.concatenate([jnp.pad(gate, pad), jnp.pad(val, pad)], axis=-1)


def kernel(x, norm_mix, w_in, gm_ln_g, gm_ln_b, gm_w_s, gm_b_s, na_rpb, w_branch, w_out,
           norm_ffn, w_up, conv_k, conv_b, w_down, norm_final):
    b, t, d = x.shape
    assert (b, t, d) == (1, SEQ, D_MODEL)
    xs = x.reshape(t, d)
    bias, h = _prologue(na_rpb, xs, norm_mix[0].reshape(1, d))
    conv_k_p, conv_b_p = _pad_ff(conv_k), _pad_ff(conv_b.reshape(DEPTH, 1, -1))
    w_s_b = gm_w_s.astype(_BF16)
    uv0, qkv0, gt0 = 0, 2 * GM_WIDTH, 2 * GM_WIDTH + 3 * NA_WIDTH
    n_uv = (2 * GM_WIDTH // PROJ_TN) * (t // PROJ_TM)
    n_gt = (2 * D_MODEL // PROJ_TN) * (t // PROJ_TM)
    uv_sides = (
        SideCast(_cast_down_rows, w_down, D_FF_PAD // n_uv, D_FF_PAD, d),
        SideCast(_cast_rows, w_branch, w_branch.shape[1] // n_uv, w_branch.shape[1], d),
        SideCast(_cast_rows, w_out, d // n_uv, d, d),
    )
    gt_sides = (SideCast(_cast_up_rows, w_up, d // n_gt, d, 2 * D_FF_PAD),)
    for l in range(DEPTH):
        uv, w_down_b, w_branch_b, w_out_b = _proj(h, w_in, l, uv0, 2 * GM_WIDTH, "gelu",
                                                  "proj_uv", uv_sides)
        qkv, = _proj(h, w_in, l, qkv0, 3 * NA_WIDTH, "none", "proj_qkv")
        gates, w_up_b = _proj(h, w_in, l, gt0, 2 * D_MODEL, "sigmoid", "proj_gates", gt_sides)
        yb = _na(qkv, bias, l)
        xs, h = _merge(xs, uv, gm_ln_g[l].reshape(1, -1), gm_ln_b[l].reshape(1, -1), w_s_b[l],
                       gm_b_s[l].reshape(GM_GROUPS, CHUNK, 1), yb, gates, w_branch_b, w_out_b,
                       norm_ffn[l].reshape(1, d))
        if l == DEPTH - 1:
            xs, = _ffn(xs, h, w_up_b, conv_k_p, conv_b_p, w_down_b, l,
                       norm_final.reshape(1, d), True)
        else:
            xs, h = _ffn(xs, h, w_up_b, conv_k_p, conv_b_p, w_down_b, l,
                         norm_mix[l + 1].reshape(1, d), False)
    return xs.reshape(b, t, d)
```

```python
from functools import partial
from typing import Callable, NamedTuple

import jax
import jax.numpy as jnp
from jax import lax
from jax.experimental import pallas as pl
from jax.experimental.pallas import tpu as pltpu

D_MODEL = 2048
SEQ = 8192
DEPTH = 2
GRID_W = 64
ROWS = SEQ // GRID_W
GM_GROUPS = 8
GM_HEAD = 128
GM_WIDTH = GM_GROUPS * GM_HEAD
CHUNK = 128
NA_HEADS = 16
NA_HEAD_DIM = 64
NA_WIDTH = NA_HEADS * NA_HEAD_DIM
NA_KR = 8
NA_KC = 16
D_FF = 5504
RMS_EPS = 1e-6
LN_EPS = 1e-5
NEG_INF = -1e30

LANES = 128
BF16_SUBLANES = 16
VMEM_BYTES = 64 * 1024 * 1024
VMEM_LIMIT_BYTES = VMEM_BYTES - 4 * 1024 * 1024

FF_TILE = 512
D_FF_PAD = ((D_FF + FF_TILE - 1) // FF_TILE) * FF_TILE
N_FF_TILES = D_FF_PAD // FF_TILE
PROJ_TM = 1024
PROJ_TM_PLAIN = 2048
PROJ_TN = 1024
MERGE_TM = 512
FFN_TM = 1024
FFN_LAG = 1
HALO = BF16_SUBLANES
NA_KTOK = NA_KR * GRID_W
NA_ROWS_PER_STEP = 64
NA_UNROLL = 64
NA_CLASS_ROW = (0, 1, 2, 3, ROWS // 2, ROWS - 3, ROWS - 2, ROWS - 1)
N_CLASSES = len(NA_CLASS_ROW)
NA_INTERIOR_CLASS = NA_KR // 2

_BF16 = jnp.bfloat16
_F32 = jnp.float32


def _params(sem):
    return pltpu.CompilerParams(dimension_semantics=sem, vmem_limit_bytes=VMEM_LIMIT_BYTES)


def _rms(x, g):
    return x * lax.rsqrt(jnp.mean(x * x, axis=-1, keepdims=True) + RMS_EPS) * g


def _gelu_tanh(x):
    return 0.5 * x * (1.0 + jnp.tanh(0.7978845608028654 * (x + 0.044715 * (x * x * x))))


def _sigmoid(x):
    return 1.0 / (1.0 + jnp.exp(-x))


_ACTS = {"gelu": _gelu_tanh, "sigmoid": _sigmoid, "none": lambda x: x}


class SideCast(NamedTuple):
    body: Callable
    array: jax.Array
    rows: int
    out_rows: int
    out_cols: int


def _cast_rows(x_ref, o_ref, step):
    del step
    o_ref[...] = x_ref[...].astype(o_ref.dtype)


def _cast_up_rows(x_ref, o_ref, step):
    del step
    zeros = jnp.zeros((x_ref.shape[0], D_FF_PAD - D_FF), o_ref.dtype)
    o_ref[:, :D_FF] = x_ref[:, :D_FF].astype(o_ref.dtype)
    o_ref[:, D_FF:D_FF_PAD] = zeros
    o_ref[:, D_FF_PAD:D_FF_PAD + D_FF] = x_ref[:, D_FF:].astype(o_ref.dtype)
    o_ref[:, D_FF_PAD + D_FF:] = zeros


def _cast_down_rows(x_ref, o_ref, step):
    row = step * x_ref.shape[0] + lax.broadcasted_iota(jnp.int32, x_ref.shape, 0)
    o_ref[...] = jnp.where(row < D_FF, x_ref[...], 0.0).astype(o_ref.dtype)


def _proj_kernel(*refs, act, side_bodies):
    n = len(side_bodies)
    h_ref, w_ref = refs[:2]
    side_in = refs[2:2 + n]
    o_ref = refs[2 + n]
    side_out = refs[3 + n:3 + 2 * n]
    w_scr = refs[3 + 2 * n]

    @pl.when(pl.program_id(1) == 0)
    def _():
        w_scr[...] = w_ref[...].astype(_BF16)

    step = pl.program_id(0) * pl.num_programs(1) + pl.program_id(1)
    for body, x_ref, y_ref in zip(side_bodies, side_in, side_out):
        body(x_ref, y_ref, step)

    acc = jnp.dot(h_ref[...], w_scr[...], preferred_element_type=_F32)
    o_ref[...] = _ACTS[act](acc).astype(o_ref.dtype)


def _proj(h, w_in, layer, col0, ncols, act, name, sides=(), tm=PROJ_TM):
    m = h.shape[0]
    jb0 = col0 // PROJ_TN
    n_i = m // tm
    grid = (ncols // PROJ_TN, n_i)
    slab = lambda j, i: (j * n_i + i, 0)
    for s in sides:
        assert s.rows * grid[0] * grid[1] == s.out_rows, (name, s.rows, s.out_rows)
    outs = pl.pallas_call(
        partial(_proj_kernel, act=act, side_bodies=tuple(s.body for s in sides)),
        grid=grid,
        in_specs=[
            pl.BlockSpec((tm, D_MODEL), lambda j, i: (i, 0)),
            pl.BlockSpec((None, D_MODEL, PROJ_TN), lambda j, i: (layer, 0, jb0 + j)),
        ] + [pl.BlockSpec((None, s.rows, s.array.shape[2]), lambda j, i: (layer, j * n_i + i, 0))
             for s in sides],
        out_specs=[pl.BlockSpec((tm, PROJ_TN), lambda j, i: (i, j))]
        + [pl.BlockSpec((s.rows, s.out_cols), slab) for s in sides],
        out_shape=[jax.ShapeDtypeStruct((m, ncols), _BF16)]
        + [jax.ShapeDtypeStruct((s.out_rows, s.out_cols), _BF16) for s in sides],
        scratch_shapes=[pltpu.VMEM((D_MODEL, PROJ_TN), _BF16)],
        compiler_params=_params(("arbitrary", "arbitrary")),
        name=name,
    )(h, w_in, *[s.array for s in sides])
    return outs


def _gmlp_gate(uv_ref, lng_ref, lnb_ref, ws_ref, bs_ref, ya_ref):
    v = uv_ref[:, GM_WIDTH:].astype(_F32)
    mu = jnp.mean(v, axis=-1, keepdims=True)
    vc = v - mu
    var = jnp.mean(vc * vc, axis=-1, keepdims=True)
    vn = (vc * lax.rsqrt(var + LN_EPS) * lng_ref[...] + lnb_ref[...]).astype(_BF16)
    for n in range(uv_ref.shape[0] // CHUNK):
        rows = slice(n * CHUNK, (n + 1) * CHUNK)
        for g in range(GM_GROUPS):
            cols = slice(g * GM_HEAD, (g + 1) * GM_HEAD)
            mixed = jnp.dot(ws_ref[g], vn[rows, cols], preferred_element_type=_F32) + bs_ref[g]
            ya_ref[rows, cols] = (uv_ref[rows, cols].astype(_F32) * mixed).astype(ya_ref.dtype)


RPB_LANE0 = LANES // 2 - (NA_KC - 1)


def _prologue_kernel(rpb_ref, x_ref, g_ref, b_ref, h_ref):
    h_ref[...] = _rms(x_ref[...], g_ref[...]).astype(h_ref.dtype)

    c = lax.broadcasted_iota(jnp.int32, (GRID_W, LANES), 0)
    lane = lax.broadcasted_iota(jnp.int32, (GRID_W, LANES), 1)
    first = lane < GRID_W
    k = jnp.where(first, lane, lane - GRID_W)
    cs = jnp.clip(c - NA_KC // 2, 0, GRID_W - NA_KC)
    in_win = (k >= cs) & (k < cs + NA_KC)

    def toeplitz(dr, shift):
        row = jnp.broadcast_to(rpb_ref[0, dr:dr + 1, :], (GRID_W, LANES))
        return pltpu.roll(row, shift, axis=1, stride=1, stride_axis=0)

    n_dr = 2 * NA_KR - 1
    pairs = [jnp.where(in_win, jnp.where(first, toeplitz(dr, LANES // 2), toeplitz(dr + 1, 0)),
                       NEG_INF) for dr in range(n_dr - 1)]
    for cls, r in enumerate(NA_CLASS_ROW):
        rs = min(max(r - NA_KR // 2, 0), ROWS - NA_KR)
        for i in range(0, NA_KR, 2):
            b_ref[0, cls, :, i * GRID_W:(i + 2) * GRID_W] = pairs[rs + i - r + NA_KR - 1]


def _prologue(rpb, x, g):
    n = DEPTH * NA_HEADS
    m = x.shape[0]
    rows = m // n
    rpb_p = jnp.pad(rpb.reshape(n, 2 * NA_KR - 1, 2 * NA_KC - 1),
                    ((0, 0), (0, 1), (RPB_LANE0, LANES - RPB_LANE0 - (2 * NA_KC - 1))))
    return pl.pallas_call(
        _prologue_kernel,
        grid=(n,),
        in_specs=[pl.BlockSpec((1, 2 * NA_KR, LANES), lambda s: (s, 0, 0)),
                  pl.BlockSpec((rows, D_MODEL), lambda s: (s, 0)),
                  pl.BlockSpec((1, D_MODEL), lambda s: (0, 0))],
        out_specs=[pl.BlockSpec((1, N_CLASSES, GRID_W, NA_KTOK), lambda s: (s // 2, 0, s % 2, 0)),
                   pl.BlockSpec((rows, D_MODEL), lambda s: (s, 0))],
        out_shape=[jax.ShapeDtypeStruct((n // 2, N_CLASSES, 2 * GRID_W, NA_KTOK), _F32),
                   jax.ShapeDtypeStruct((m, D_MODEL), _BF16)],
        compiler_params=_params(("arbitrary",)),
        name="prologue",
    )(rpb_p, x, g)


def _na_window(r):
    rs = jnp.clip(r - NA_KR // 2, 0, ROWS - NA_KR)
    cls = jnp.where(r < NA_INTERIOR_CLASS, r,
                    jnp.where(r > ROWS - NA_KR // 2, r - (ROWS - NA_KR), NA_INTERIOR_CLASS))
    return pl.ds(pl.multiple_of(rs * GRID_W, GRID_W), NA_KTOK), cls


def _na_kernel(q_ref, k_ref, v_ref, b_ref, o_ref, p_scr, inv_scr):
    rb = pl.program_id(1)
    n_rb = pl.num_programs(1) - 1
    scale = NA_HEAD_DIM ** -0.5
    head0 = lax.broadcasted_iota(jnp.int32, (GRID_W, LANES), 1) < NA_HEAD_DIM

    def run(scores, apply):
        def group(g, carry):
            for u in range(NA_UNROLL):
                t = g * NA_UNROLL + u
                qrows = pl.ds(pl.multiple_of(t * GRID_W, GRID_W), GRID_W)
                if scores:
                    krows, cls = _na_window(rb * NA_ROWS_PER_STEP + t)
                    q = q_ref[qrows, :] * scale
                    zero = jnp.zeros_like(q)
                    q2 = jnp.concatenate([jnp.where(head0, q, zero), jnp.where(head0, zero, q)],
                                         axis=0)
                    s = lax.dot_general(q2, k_ref[krows, :], (((1,), (1,)), ((), ())),
                                        preferred_element_type=_F32) + b_ref[0, cls]
                    p = jnp.exp(s - jnp.max(s, axis=-1, keepdims=True))
                    den = jnp.sum(p, axis=-1, keepdims=True)
                if apply:
                    vrows, _ = _na_window((rb - 1) * NA_ROWS_PER_STEP + t)
                    o = jnp.dot(p_scr[t], v_ref[vrows, :],
                                preferred_element_type=_F32) * inv_scr[t]
                    o_ref[qrows, :] = jnp.where(head0, o[:GRID_W], o[GRID_W:]).astype(o_ref.dtype)
                if scores:
                    p_scr[t] = p.astype(_BF16)
                    inv_scr[t] = jnp.broadcast_to(1.0 / den, (2 * GRID_W, LANES))
            return carry

        lax.fori_loop(0, NA_ROWS_PER_STEP // NA_UNROLL, group, 0)

    pl.when(rb == 0)(lambda: run(True, False))
    pl.when((rb > 0) & (rb < n_rb))(lambda: run(True, True))
    pl.when(rb == n_rb)(lambda: run(False, True))


def _na(qkv, bias, layer):
    m = qkv.shape[0]
    tq = NA_ROWS_PER_STEP * GRID_W
    n_hp = NA_WIDTH // LANES
    n_rb = m // tq
    return pl.pallas_call(
        _na_kernel,
        grid=(n_hp, n_rb + 1),
        in_specs=[
            pl.BlockSpec((tq, LANES), lambda hp, rb: (jnp.minimum(rb, n_rb - 1), hp)),
            pl.BlockSpec((m, LANES), lambda hp, rb: (0, n_hp + hp)),
            pl.BlockSpec((m, LANES), lambda hp, rb: (0, 2 * n_hp + hp)),
            pl.BlockSpec((1, N_CLASSES, 2 * GRID_W, NA_KTOK),
                         lambda hp, rb: (layer * n_hp + hp, 0, 0, 0)),
        ],
        out_specs=pl.BlockSpec((tq, LANES), lambda hp, rb: (jnp.maximum(rb - 1, 0), hp)),
        out_shape=jax.ShapeDtypeStruct((m, NA_WIDTH), _BF16),
        scratch_shapes=[
            pltpu.VMEM((NA_ROWS_PER_STEP, 2 * GRID_W, NA_KTOK), _BF16),
            pltpu.VMEM((NA_ROWS_PER_STEP, 2 * GRID_W, LANES), _F32),
        ],
        compiler_params=_params(("arbitrary", "arbitrary")),
        name="na_attn",
    )(qkv, qkv, qkv, bias)


def _merge_kernel(x_ref, uv_ref, lng_ref, lnb_ref, ws_ref, bs_ref, yb_ref, gt_ref, wb_ref, wo_ref,
                  g_ref, o_ref, h_ref, ya_scr):
    _gmlp_gate(uv_ref, lng_ref, lnb_ref, ws_ref, bs_ref, ya_scr)
    ta = jnp.dot(ya_scr[...], wb_ref[:GM_WIDTH, :], preferred_element_type=_F32)
    tb = jnp.dot(yb_ref[...], wb_ref[GM_WIDTH:, :], preferred_element_type=_F32)
    merged = (gt_ref[:, :D_MODEL].astype(_F32) * ta + gt_ref[:, D_MODEL:].astype(_F32) * tb)
    y = x_ref[...] + jnp.dot(merged.astype(_BF16), wo_ref[...], preferred_element_type=_F32)
    o_ref[...] = y
    h_ref[...] = _rms(y, g_ref[...]).astype(h_ref.dtype)


def _merge(x, uv, ln_g, ln_b, w_s, b_s, yb, gates, w_branch, w_out, g_ffn):
    m = x.shape[0]
    resident = dict(pipeline_mode=pl.Buffered(1))
    row = lambda i: (i, 0)
    fixed2 = lambda i: (0, 0)
    fixed3 = lambda i: (0, 0, 0)
    return pl.pallas_call(
        _merge_kernel,
        grid=(m // MERGE_TM,),
        in_specs=[
            pl.BlockSpec((MERGE_TM, D_MODEL), row),
            pl.BlockSpec((MERGE_TM, 2 * GM_WIDTH), row),
            pl.BlockSpec((1, GM_WIDTH), fixed2),
            pl.BlockSpec((1, GM_WIDTH), fixed2),
            pl.BlockSpec((GM_GROUPS, CHUNK, CHUNK), fixed3),
            pl.BlockSpec((GM_GROUPS, CHUNK, 1), fixed3),
            pl.BlockSpec((MERGE_TM, NA_WIDTH), row),
            pl.BlockSpec((MERGE_TM, 2 * D_MODEL), row),
            pl.BlockSpec((GM_WIDTH + NA_WIDTH, D_MODEL), fixed2, **resident),
            pl.BlockSpec((D_MODEL, D_MODEL), fixed2, **resident),
            pl.BlockSpec((1, D_MODEL), fixed2),
        ],
        out_specs=[pl.BlockSpec((MERGE_TM, D_MODEL), row), pl.BlockSpec((MERGE_TM, D_MODEL), row)],
        out_shape=[jax.ShapeDtypeStruct((m, D_MODEL), _F32),
                   jax.ShapeDtypeStruct((m, D_MODEL), _BF16)],
        scratch_shapes=[pltpu.VMEM((MERGE_TM, GM_WIDTH), _BF16)],
        compiler_params=_params(("arbitrary",)),
        name="merge_out",
    )(x, uv, ln_g, ln_b, w_s, b_s, yb, gates, w_branch, w_out, g_ffn)


def _ffn_kernel(x_hbm, h_ref, hp_ref, hn_ref, wg_ref, wv_ref, kg_ref, kv_ref, bg_ref, bv_ref,
                wd_ref, gn_ref, *refs, final_norm):
    if final_norm:
        o_ref, h_scr, ug_scr, uv_scr, act_scr, x_sem = refs
    else:
        o_ref, ho_ref, h_scr, ug_scr, uv_scr, act_scr, x_sem = refs
    i, j = pl.program_id(0), pl.program_id(1)
    tm = FFN_TM

    def x_copy():
        rows = pl.ds(pl.multiple_of(i * tm, tm), tm)
        return pltpu.make_async_copy(x_hbm.at[rows, :], o_ref, x_sem)

    def up_dots():
        h = h_scr[...]
        for u_scr, w_ref in ((ug_scr, wg_ref), (uv_scr, wv_ref)):
            u = jnp.dot(h, w_ref[...], preferred_element_type=_F32)
            for c in range(FF_TILE // LANES):
                u_scr[c] = u[:, c * LANES:(c + 1) * LANES]

    def conv(u_scr, k_ref, b_ref, c):
        cols = slice(c * LANES, (c + 1) * LANES)
        return (u_scr[c, HALO - 1:HALO - 1 + tm, :] * k_ref[0:1, cols]
                + u_scr[c, HALO:HALO + tm, :] * k_ref[1:2, cols]
                + u_scr[c, HALO + 1:HALO + 1 + tm, :] * k_ref[2:3, cols] + b_ref[:, cols])

    def gate_into(slot):
        for c in range(FF_TILE // LANES):
            gate = conv(ug_scr, kg_ref, bg_ref, c)
            val = conv(uv_scr, kv_ref, bv_ref, c)
            act_scr[slot, :, c * LANES:(c + 1) * LANES] = (
                gate * _sigmoid(gate) * val).astype(_BF16)

    def down_from(slot):
        o_ref[...] += jnp.dot(act_scr[slot], wd_ref[...], preferred_element_type=_F32)

    @pl.when(j == 0)
    def _():
        zero = jnp.zeros((HALO, D_MODEL), _BF16)
        h_scr[:HALO, :] = jnp.where(i > 0, hp_ref[...], zero)
        h_scr[HALO:HALO + tm, :] = h_ref[...]
        h_scr[HALO + tm:, :] = jnp.where(i < pl.num_programs(0) - 1, hn_ref[...], zero)

    slots = FFN_LAG + 1
    for s in range(FFN_LAG):
        @pl.when(j == s)
        def _(s=s):
            if s == FFN_LAG - 1:
                x_copy().start()
            up_dots()
            gate_into(s)

    @pl.when(j == FFN_LAG)
    def _():
        x_copy().wait()

    for r in range(slots):
        @pl.when((j >= FFN_LAG) & (j < N_FF_TILES) & (j % slots == r))
        def _(r=r):
            up_dots()
            down_from((r - FFN_LAG) % slots)
            gate_into(r)

    for s in range(N_FF_TILES, N_FF_TILES + FFN_LAG):
        @pl.when(j == s)
        def _(s=s):
            down_from((s - FFN_LAG) % slots)
            if s == N_FF_TILES + FFN_LAG - 1:
                hn = _rms(o_ref[...], gn_ref[...])
                if final_norm:
                    o_ref[...] = hn
                else:
                    ho_ref[...] = hn.astype(ho_ref.dtype)


def _ffn(x, h, w_up, conv_k, conv_b, w_down, layer, g_next, final_norm):
    m = x.shape[0]
    tm = FFN_TM
    nb = tm // HALO
    last = m // HALO - 1
    row = lambda i, j: (i, 0)
    up_tile = lambda j: jnp.minimum(j, N_FF_TILES - 1)
    gate = lambda i, j: (layer, 0, up_tile(j))
    val = lambda i, j: (layer, 0, N_FF_TILES + up_tile(j))
    out_specs = [pl.BlockSpec((tm, D_MODEL), row)]
    out_shape = [jax.ShapeDtypeStruct((m, D_MODEL), _F32)]
    if not final_norm:
        out_specs.append(pl.BlockSpec((tm, D_MODEL), row))
        out_shape.append(jax.ShapeDtypeStruct((m, D_MODEL), _BF16))
    return pl.pallas_call(
        partial(_ffn_kernel, final_norm=final_norm),
        grid=(m // tm, N_FF_TILES + FFN_LAG),
        in_specs=[
            pl.BlockSpec(memory_space=pl.ANY),
            pl.BlockSpec((tm, D_MODEL), row),
            pl.BlockSpec((HALO, D_MODEL), lambda i, j: (jnp.maximum(i * nb - 1, 0), 0)),
            pl.BlockSpec((HALO, D_MODEL), lambda i, j: (jnp.minimum((i + 1) * nb, last), 0)),
            pl.BlockSpec((D_MODEL, FF_TILE), lambda i, j: (0, up_tile(j))),
            pl.BlockSpec((D_MODEL, FF_TILE), lambda i, j: (0, N_FF_TILES + up_tile(j))),
            pl.BlockSpec((None, 3, FF_TILE), gate),
            pl.BlockSpec((None, 3, FF_TILE), val),
            pl.BlockSpec((None, 1, FF_TILE), gate),
            pl.BlockSpec((None, 1, FF_TILE), val),
            pl.BlockSpec((FF_TILE, D_MODEL),
                         lambda i, j: (jnp.clip(j - FFN_LAG, 0, N_FF_TILES - 1), 0)),
            pl.BlockSpec((1, D_MODEL), lambda i, j: (0, 0)),
        ],
        out_specs=out_specs,
        out_shape=out_shape,
        scratch_shapes=[
            pltpu.VMEM((tm + 2 * HALO, D_MODEL), _BF16),
            pltpu.VMEM((FF_TILE // LANES, tm + 2 * HALO, LANES), _F32),
            pltpu.VMEM((FF_TILE // LANES, tm + 2 * HALO, LANES), _F32),
            pltpu.VMEM((FFN_LAG + 1, tm, FF_TILE), _BF16),
            pltpu.SemaphoreType.DMA(()),
        ],
        compiler_params=_params(("arbitrary", "arbitrary")),
        name="conv_ffn",
    )(x, h, h, h, w_up, w_up, conv_k, conv_k, conv_b, conv_b, w_down, g_next)


def _pad_ff(a):
    gate, val = jnp.split(a, 2, axis=-1)
    pad = [(0, 0)] * (a.ndim - 1) + [(0, D_FF_PAD - D_FF)]
    return jnp.concatenate([jnp.pad(gate, pad), jnp.pad(val, pad)], axis=-1)


def kernel(x, norm_mix, w_in, gm_ln_g, gm_ln_b, gm_w_s, gm_b_s, na_rpb, w_branch, w_out,
           norm_ffn, w_up, conv_k, conv_b, w_down, norm_final):
    b, t, d = x.shape
    assert (b, t, d) == (1, SEQ, D_MODEL)
    xs = x.reshape(t, d)
    bias, h = _prologue(na_rpb, xs, norm_mix[0].reshape(1, d))
    conv_k_p, conv_b_p = _pad_ff(conv_k), _pad_ff(conv_b.reshape(DEPTH, 1, -1))
    w_s_b = gm_w_s.astype(_BF16)
    uv0, qkv0, gt0 = 0, 2 * GM_WIDTH, 2 * GM_WIDTH + 3 * NA_WIDTH
    n_uv = (2 * GM_WIDTH // PROJ_TN) * (t // PROJ_TM)
    n_gt = (2 * D_MODEL // PROJ_TN) * (t // PROJ_TM)
    uv_sides = (
        SideCast(_cast_down_rows, w_down, D_FF_PAD // n_uv, D_FF_PAD, d),
        SideCast(_cast_rows, w_branch, w_branch.shape[1] // n_uv, w_branch.shape[1], d),
        SideCast(_cast_rows, w_out, d // n_uv, d, d),
    )
    gt_sides = (SideCast(_cast_up_rows, w_up, d // n_gt, d, 2 * D_FF_PAD),)
    for l in range(DEPTH):
        uv, w_down_b, w_branch_b, w_out_b = _proj(h, w_in, l, uv0, 2 * GM_WIDTH, "gelu",
                                                  "proj_uv", uv_sides)
        qkv, = _proj(h, w_in, l, qkv0, 3 * NA_WIDTH, "none", "proj_qkv", tm=PROJ_TM_PLAIN)
        gates, w_up_b = _proj(h, w_in, l, gt0, 2 * D_MODEL, "sigmoid", "proj_gates", gt_sides)
        yb = _na(qkv, bias, l)
        xs, h = _merge(xs, uv, gm_ln_g[l].reshape(1, -1), gm_ln_b[l].reshape(1, -1), w_s_b[l],
                       gm_b_s[l].reshape(GM_GROUPS, CHUNK, 1), yb, gates, w_branch_b, w_out_b,
                       norm_ffn[l].reshape(1, d))
        if l == DEPTH - 1:
            xs, = _ffn(xs, h, w_up_b, conv_k_p, conv_b_p, w_down_b, l,
                       norm_final.reshape(1, d), True)
        else:
            xs, h = _ffn(xs, h, w_up_b, conv_k_p, conv_b_p, w_down_b, l,
                         norm_mix[l + 1].reshape(1, d), False)
    return xs.reshape(b, t, d)
```

```python
from functools import partial
from typing import Callable, NamedTuple

import jax
import jax.numpy as jnp
from jax import lax
from jax.experimental import pallas as pl
from jax.experimental.pallas import tpu as pltpu

D_MODEL = 2048
SEQ = 8192
DEPTH = 2
GRID_W = 64
ROWS = SEQ // GRID_W
GM_GROUPS = 8
GM_HEAD = 128
GM_WIDTH = GM_GROUPS * GM_HEAD
CHUNK = 128
NA_HEADS = 16
NA_HEAD_DIM = 64
NA_WIDTH = NA_HEADS * NA_HEAD_DIM
NA_KR = 8
NA_KC = 16
D_FF = 5504
RMS_EPS = 1e-6
LN_EPS = 1e-5
NEG_INF = -1e30

LANES = 128
BF16_SUBLANES = 16
VMEM_BYTES = 64 * 1024 * 1024
VMEM_LIMIT_BYTES = VMEM_BYTES - 4 * 1024 * 1024

FF_TILE = 512
D_FF_PAD = ((D_FF + FF_TILE - 1) // FF_TILE) * FF_TILE
N_FF_TILES = D_FF_PAD // FF_TILE
PROJ_TM = 1024
PROJ_TM_PLAIN = 2048
PROJ_TN = 1024
MERGE_TM = 512
FFN_TM = 1024
FFN_LAG = 1
HALO = BF16_SUBLANES
NA_KTOK = NA_KR * GRID_W
NA_ROWS_PER_STEP = 64
NA_PV_LAG = 3
NA_CLASS_ROW = (0, 1, 2, 3, ROWS // 2, ROWS - 3, ROWS - 2, ROWS - 1)
N_CLASSES = len(NA_CLASS_ROW)
NA_INTERIOR_CLASS = NA_KR // 2

_BF16 = jnp.bfloat16
_F32 = jnp.float32


def _params(sem):
    return pltpu.CompilerParams(dimension_semantics=sem, vmem_limit_bytes=VMEM_LIMIT_BYTES)


def _rms(x, g):
    return x * lax.rsqrt(jnp.mean(x * x, axis=-1, keepdims=True) + RMS_EPS) * g


def _gelu_tanh(x):
    return 0.5 * x * (1.0 + jnp.tanh(0.7978845608028654 * (x + 0.044715 * (x * x * x))))


def _sigmoid(x):
    return 1.0 / (1.0 + jnp.exp(-x))


_ACTS = {"gelu": _gelu_tanh, "sigmoid": _sigmoid, "none": lambda x: x}


class SideCast(NamedTuple):
    body: Callable
    array: jax.Array
    rows: int
    out_rows: int
    out_cols: int


def _cast_rows(x_ref, o_ref, step):
    del step
    o_ref[...] = x_ref[...].astype(o_ref.dtype)


def _cast_up_rows(x_ref, o_ref, step):
    del step
    zeros = jnp.zeros((x_ref.shape[0], D_FF_PAD - D_FF), o_ref.dtype)
    o_ref[:, :D_FF] = x_ref[:, :D_FF].astype(o_ref.dtype)
    o_ref[:, D_FF:D_FF_PAD] = zeros
    o_ref[:, D_FF_PAD:D_FF_PAD + D_FF] = x_ref[:, D_FF:].astype(o_ref.dtype)
    o_ref[:, D_FF_PAD + D_FF:] = zeros


def _cast_down_rows(x_ref, o_ref, step):
    row = step * x_ref.shape[0] + lax.broadcasted_iota(jnp.int32, x_ref.shape, 0)
    o_ref[...] = jnp.where(row < D_FF, x_ref[...], 0.0).astype(o_ref.dtype)


def _proj_kernel(*refs, act, side_bodies):
    n = len(side_bodies)
    h_ref, w_ref = refs[:2]
    side_in = refs[2:2 + n]
    o_ref = refs[2 + n]
    side_out = refs[3 + n:3 + 2 * n]
    w_scr = refs[3 + 2 * n]

    @pl.when(pl.program_id(1) == 0)
    def _():
        w_scr[...] = w_ref[...].astype(_BF16)

    step = pl.program_id(0) * pl.num_programs(1) + pl.program_id(1)
    for body, x_ref, y_ref in zip(side_bodies, side_in, side_out):
        body(x_ref, y_ref, step)

    acc = jnp.dot(h_ref[...], w_scr[...], preferred_element_type=_F32)
    o_ref[...] = _ACTS[act](acc).astype(o_ref.dtype)


def _proj(h, w_in, layer, col0, ncols, act, name, sides=(), tm=PROJ_TM):
    m = h.shape[0]
    jb0 = col0 // PROJ_TN
    n_i = m // tm
    grid = (ncols // PROJ_TN, n_i)
    slab = lambda j, i: (j * n_i + i, 0)
    for s in sides:
        assert s.rows * grid[0] * grid[1] == s.out_rows, (name, s.rows, s.out_rows)
    outs = pl.pallas_call(
        partial(_proj_kernel, act=act, side_bodies=tuple(s.body for s in sides)),
        grid=grid,
        in_specs=[
            pl.BlockSpec((tm, D_MODEL), lambda j, i: (i, 0)),
            pl.BlockSpec((None, D_MODEL, PROJ_TN), lambda j, i: (layer, 0, jb0 + j)),
        ] + [pl.BlockSpec((None, s.rows, s.array.shape[2]), lambda j, i: (layer, j * n_i + i, 0))
             for s in sides],
        out_specs=[pl.BlockSpec((tm, PROJ_TN), lambda j, i: (i, j))]
        + [pl.BlockSpec((s.rows, s.out_cols), slab) for s in sides],
        out_shape=[jax.ShapeDtypeStruct((m, ncols), _BF16)]
        + [jax.ShapeDtypeStruct((s.out_rows, s.out_cols), _BF16) for s in sides],
        scratch_shapes=[pltpu.VMEM((D_MODEL, PROJ_TN), _BF16)],
        compiler_params=_params(("arbitrary", "arbitrary")),
        name=name,
    )(h, w_in, *[s.array for s in sides])
    return outs


def _gmlp_gate(uv_ref, lng_ref, lnb_ref, ws_ref, bs_ref, ya_ref):
    v = uv_ref[:, GM_WIDTH:].astype(_F32)
    mu = jnp.mean(v, axis=-1, keepdims=True)
    vc = v - mu
    var = jnp.mean(vc * vc, axis=-1, keepdims=True)
    vn = (vc * lax.rsqrt(var + LN_EPS) * lng_ref[...] + lnb_ref[...]).astype(_BF16)
    for n in range(uv_ref.shape[0] // CHUNK):
        rows = slice(n * CHUNK, (n + 1) * CHUNK)
        for g in range(GM_GROUPS):
            cols = slice(g * GM_HEAD, (g + 1) * GM_HEAD)
            mixed = jnp.dot(ws_ref[g], vn[rows, cols], preferred_element_type=_F32) + bs_ref[g]
            ya_ref[rows, cols] = (uv_ref[rows, cols].astype(_F32) * mixed).astype(ya_ref.dtype)


RPB_LANE0 = LANES // 2 - (NA_KC - 1)


def _prologue_kernel(rpb_ref, x_ref, g_ref, b_ref, h_ref):
    h_ref[...] = _rms(x_ref[...], g_ref[...]).astype(h_ref.dtype)

    c = lax.broadcasted_iota(jnp.int32, (GRID_W, LANES), 0)
    lane = lax.broadcasted_iota(jnp.int32, (GRID_W, LANES), 1)
    first = lane < GRID_W
    k = jnp.where(first, lane, lane - GRID_W)
    cs = jnp.clip(c - NA_KC // 2, 0, GRID_W - NA_KC)
    in_win = (k >= cs) & (k < cs + NA_KC)

    def toeplitz(dr, shift):
        row = jnp.broadcast_to(rpb_ref[0, dr:dr + 1, :], (GRID_W, LANES))
        return pltpu.roll(row, shift, axis=1, stride=1, stride_axis=0)

    n_dr = 2 * NA_KR - 1
    pairs = [jnp.where(in_win, jnp.where(first, toeplitz(dr, LANES // 2), toeplitz(dr + 1, 0)),
                       NEG_INF) for dr in range(n_dr - 1)]
    for cls, r in enumerate(NA_CLASS_ROW):
        rs = min(max(r - NA_KR // 2, 0), ROWS - NA_KR)
        for i in range(0, NA_KR, 2):
            b_ref[0, cls, :, i * GRID_W:(i + 2) * GRID_W] = pairs[rs + i - r + NA_KR - 1]


def _prologue(rpb, x, g):
    n = DEPTH * NA_HEADS
    m = x.shape[0]
    rows = m // n
    rpb_p = jnp.pad(rpb.reshape(n, 2 * NA_KR - 1, 2 * NA_KC - 1),
                    ((0, 0), (0, 1), (RPB_LANE0, LANES - RPB_LANE0 - (2 * NA_KC - 1))))
    return pl.pallas_call(
        _prologue_kernel,
        grid=(n,),
        in_specs=[pl.BlockSpec((1, 2 * NA_KR, LANES), lambda s: (s, 0, 0)),
                  pl.BlockSpec((rows, D_MODEL), lambda s: (s, 0)),
                  pl.BlockSpec((1, D_MODEL), lambda s: (0, 0))],
        out_specs=[pl.BlockSpec((1, N_CLASSES, GRID_W, NA_KTOK), lambda s: (s // 2, 0, s % 2, 0)),
                   pl.BlockSpec((rows, D_MODEL), lambda s: (s, 0))],
        out_shape=[jax.ShapeDtypeStruct((n // 2, N_CLASSES, 2 * GRID_W, NA_KTOK), _F32),
                   jax.ShapeDtypeStruct((m, D_MODEL), _BF16)],
        compiler_params=_params(("arbitrary",)),
        name="prologue",
    )(rpb_p, x, g)


def _na_window(r):
    rs = jnp.clip(r - NA_KR // 2, 0, ROWS - NA_KR)
    cls = jnp.where(r < NA_INTERIOR_CLASS, r,
                    jnp.where(r > ROWS - NA_KR // 2, r - (ROWS - NA_KR), NA_INTERIOR_CLASS))
    return pl.ds(pl.multiple_of(rs * GRID_W, GRID_W), NA_KTOK), cls


def _na_kernel(q_ref, k_ref, v_ref, b_ref, o_ref):
    rb = pl.program_id(1)
    scale = NA_HEAD_DIM ** -0.5
    head0 = lax.broadcasted_iota(jnp.int32, (GRID_W, LANES), 1) < NA_HEAD_DIM

    def apply(t, p, inv):
        qrows = slice(t * GRID_W, (t + 1) * GRID_W)
        vrows, _ = _na_window(rb * NA_ROWS_PER_STEP + t)
        o = jnp.dot(p, v_ref[vrows, :], preferred_element_type=_F32) * inv
        o_ref[qrows, :] = jnp.where(head0, o[:GRID_W], o[GRID_W:]).astype(o_ref.dtype)

    pending = []
    for t in range(NA_ROWS_PER_STEP):
        krows, cls = _na_window(rb * NA_ROWS_PER_STEP + t)
        q = q_ref[t * GRID_W:(t + 1) * GRID_W, :] * scale
        zero = jnp.zeros_like(q)
        q2 = jnp.concatenate([jnp.where(head0, q, zero), jnp.where(head0, zero, q)], axis=0)
        s = lax.dot_general(q2, k_ref[krows, :], (((1,), (1,)), ((), ())),
                            preferred_element_type=_F32) + b_ref[0, cls]
        p = jnp.exp(s - jnp.max(s, axis=-1, keepdims=True))
        den = jnp.sum(p, axis=-1, keepdims=True)
        pending.append((t, p.astype(_BF16), 1.0 / den))
        if len(pending) > NA_PV_LAG:
            apply(*pending.pop(0))
    for item in pending:
        apply(*item)


def _na(qkv, bias, layer):
    m = qkv.shape[0]
    tq = NA_ROWS_PER_STEP * GRID_W
    n_hp = NA_WIDTH // LANES
    return pl.pallas_call(
        _na_kernel,
        grid=(n_hp, m // tq),
        in_specs=[
            pl.BlockSpec((tq, LANES), lambda hp, rb: (rb, hp)),
            pl.BlockSpec((m, LANES), lambda hp, rb: (0, n_hp + hp)),
            pl.BlockSpec((m, LANES), lambda hp, rb: (0, 2 * n_hp + hp)),
            pl.BlockSpec((1, N_CLASSES, 2 * GRID_W, NA_KTOK),
                         lambda hp, rb: (layer * n_hp + hp, 0, 0, 0)),
        ],
        out_specs=pl.BlockSpec((tq, LANES), lambda hp, rb: (rb, hp)),
        out_shape=jax.ShapeDtypeStruct((m, NA_WIDTH), _BF16),
        compiler_params=_params(("arbitrary", "arbitrary")),
        name="na_attn",
    )(qkv, qkv, qkv, bias)


def _merge_kernel(x_ref, uv_ref, lng_ref, lnb_ref, ws_ref, bs_ref, yb_ref, gt_ref, wb_ref, wo_ref,
                  g_ref, o_ref, h_ref, ya_scr):
    _gmlp_gate(uv_ref, lng_ref, lnb_ref, ws_ref, bs_ref, ya_scr)
    ta = jnp.dot(ya_scr[...], wb_ref[:GM_WIDTH, :], preferred_element_type=_F32)
    tb = jnp.dot(yb_ref[...], wb_ref[GM_WIDTH:, :], preferred_element_type=_F32)
    merged = (gt_ref[:, :D_MODEL].astype(_F32) * ta + gt_ref[:, D_MODEL:].astype(_F32) * tb)
    y = x_ref[...] + jnp.dot(merged.astype(_BF16), wo_ref[...], preferred_element_type=_F32)
    o_ref[...] = y
    h_ref[...] = _rms(y, g_ref[...]).astype(h_ref.dtype)


def _merge(x, uv, ln_g, ln_b, w_s, b_s, yb, gates, w_branch, w_out, g_ffn):
    m = x.shape[0]
    resident = dict(pipeline_mode=pl.Buffered(1))
    row = lambda i: (i, 0)
    fixed2 = lambda i: (0, 0)
    fixed3 = lambda i: (0, 0, 0)
    return pl.pallas_call(
        _merge_kernel,
        grid=(m // MERGE_TM,),
        in_specs=[
            pl.BlockSpec((MERGE_TM, D_MODEL), row),
            pl.BlockSpec((MERGE_TM, 2 * GM_WIDTH), row),
            pl.BlockSpec((1, GM_WIDTH), fixed2),
            pl.BlockSpec((1, GM_WIDTH), fixed2),
            pl.BlockSpec((GM_GROUPS, CHUNK, CHUNK), fixed3),
            pl.BlockSpec((GM_GROUPS, CHUNK, 1), fixed3),
            pl.BlockSpec((MERGE_TM, NA_WIDTH), row),
            pl.BlockSpec((MERGE_TM, 2 * D_MODEL), row),
            pl.BlockSpec((GM_WIDTH + NA_WIDTH, D_MODEL), fixed2, **resident),
            pl.BlockSpec((D_MODEL, D_MODEL), fixed2, **resident),
            pl.BlockSpec((1, D_MODEL), fixed2),
        ],
        out_specs=[pl.BlockSpec((MERGE_TM, D_MODEL), row), pl.BlockSpec((MERGE_TM, D_MODEL), row)],
        out_shape=[jax.ShapeDtypeStruct((m, D_MODEL), _F32),
                   jax.ShapeDtypeStruct((m, D_MODEL), _BF16)],
        scratch_shapes=[pltpu.VMEM((MERGE_TM, GM_WIDTH), _BF16)],
        compiler_params=_params(("arbitrary",)),
        name="merge_out",
    )(x, uv, ln_g, ln_b, w_s, b_s, yb, gates, w_branch, w_out, g_ffn)


def _ffn_kernel(x_hbm, h_ref, hp_ref, hn_ref, wg_ref, wv_ref, kg_ref, kv_ref, bg_ref, bv_ref,
                wd_ref, gn_ref, *refs, final_norm):
    if final_norm:
        o_ref, h_scr, ug_scr, uv_scr, act_scr, x_sem = refs
    else:
        o_ref, ho_ref, h_scr, ug_scr, uv_scr, act_scr, x_sem = refs
    i, j = pl.program_id(0), pl.program_id(1)
    tm = FFN_TM

    def x_copy():
        rows = pl.ds(pl.multiple_of(i * tm, tm), tm)
        return pltpu.make_async_copy(x_hbm.at[rows, :], o_ref, x_sem)

    def up_dots():
        h = h_scr[...]
        for u_scr, w_ref in ((ug_scr, wg_ref), (uv_scr, wv_ref)):
            u = jnp.dot(h, w_ref[...], preferred_element_type=_F32)
            for c in range(FF_TILE // LANES):
                u_scr[c] = u[:, c * LANES:(c + 1) * LANES]

    def conv(u_scr, k_ref, b_ref, c):
        cols = slice(c * LANES, (c + 1) * LANES)
        return (u_scr[c, HALO - 1:HALO - 1 + tm, :] * k_ref[0:1, cols]
                + u_scr[c, HALO:HALO + tm, :] * k_ref[1:2, cols]
                + u_scr[c, HALO + 1:HALO + 1 + tm, :] * k_ref[2:3, cols] + b_ref[:, cols])

    def gate_into(slot):
        for c in range(FF_TILE // LANES):
            gate = conv(ug_scr, kg_ref, bg_ref, c)
            val = conv(uv_scr, kv_ref, bv_ref, c)
            act_scr[slot, :, c * LANES:(c + 1) * LANES] = (
                gate * _sigmoid(gate) * val).astype(_BF16)

    def down_from(slot):
        o_ref[...] += jnp.dot(act_scr[slot], wd_ref[...], preferred_element_type=_F32)

    @pl.when(j == 0)
    def _():
        zero = jnp.zeros((HALO, D_MODEL), _BF16)
        h_scr[:HALO, :] = jnp.where(i > 0, hp_ref[...], zero)
        h_scr[HALO:HALO + tm, :] = h_ref[...]
        h_scr[HALO + tm:, :] = jnp.where(i < pl.num_programs(0) - 1, hn_ref[...], zero)

    slots = FFN_LAG + 1
    for s in range(FFN_LAG):
        @pl.when(j == s)
        def _(s=s):
            if s == FFN_LAG - 1:
                x_copy().start()
            up_dots()
            gate_into(s)

    @pl.when(j == FFN_LAG)
    def _():
        x_copy().wait()

    for r in range(slots):
        @pl.when((j >= FFN_LAG) & (j < N_FF_TILES) & (j % slots == r))
        def _(r=r):
            up_dots()
            down_from((r - FFN_LAG) % slots)
            gate_into(r)

    for s in range(N_FF_TILES, N_FF_TILES + FFN_LAG):
        @pl.when(j == s)
        def _(s=s):
            down_from((s - FFN_LAG) % slots)
            if s == N_FF_TILES + FFN_LAG - 1:
                hn = _rms(o_ref[...], gn_ref[...])
                if final_norm:
                    o_ref[...] = hn
                else:
                    ho_ref[...] = hn.astype(ho_ref.dtype)


def _ffn(x, h, w_up, conv_k, conv_b, w_down, layer, g_next, final_norm):
    m = x.shape[0]
    tm = FFN_TM
    nb = tm // HALO
    last = m // HALO - 1
    row = lambda i, j: (i, 0)
    up_tile = lambda j: jnp.minimum(j, N_FF_TILES - 1)
    gate = lambda i, j: (layer, 0, up_tile(j))
    val = lambda i, j: (layer, 0, N_FF_TILES + up_tile(j))
    out_specs = [pl.BlockSpec((tm, D_MODEL), row)]
    out_shape = [jax.ShapeDtypeStruct((m, D_MODEL), _F32)]
    if not final_norm:
        out_specs.append(pl.BlockSpec((tm, D_MODEL), row))
        out_shape.append(jax.ShapeDtypeStruct((m, D_MODEL), _BF16))
    return pl.pallas_call(
        partial(_ffn_kernel, final_norm=final_norm),
        grid=(m // tm, N_FF_TILES + FFN_LAG),
        in_specs=[
            pl.BlockSpec(memory_space=pl.ANY),
            pl.BlockSpec((tm, D_MODEL), row),
            pl.BlockSpec((HALO, D_MODEL), lambda i, j: (jnp.maximum(i * nb - 1, 0), 0)),
            pl.BlockSpec((HALO, D_MODEL), lambda i, j: (jnp.minimum((i + 1) * nb, last), 0)),
            pl.BlockSpec((D_MODEL, FF_TILE), lambda i, j: (0, up_tile(j))),
            pl.BlockSpec((D_MODEL, FF_TILE), lambda i, j: (0, N_FF_TILES + up_tile(j))),
            pl.BlockSpec((None, 3, FF_TILE), gate),
            pl.BlockSpec((None, 3, FF_TILE), val),
            pl.BlockSpec((None, 1, FF_TILE), gate),
            pl.BlockSpec((None, 1, FF_TILE), val),
            pl.BlockSpec((FF_TILE, D_MODEL),
                         lambda i, j: (jnp.clip(j - FFN_LAG, 0, N_FF_TILES - 1), 0)),
            pl.BlockSpec((1, D_MODEL), lambda i, j: (0, 0)),
        ],
        out_specs=out_specs,
        out_shape=out_shape,
        scratch_shapes=[
            pltpu.VMEM((tm + 2 * HALO, D_MODEL), _BF16),
            pltpu.VMEM((FF_TILE // LANES, tm + 2 * HALO, LANES), _F32),
            pltpu.VMEM((FF_TILE // LANES, tm + 2 * HALO, LANES), _F32),
            pltpu.VMEM((FFN_LAG + 1, tm, FF_TILE), _BF16),
            pltpu.SemaphoreType.DMA(()),
        ],
        compiler_params=_params(("arbitrary", "arbitrary")),
        name="conv_ffn",
    )(x, h, h, h, w_up, w_up, conv_k, conv_k, conv_b, conv_b, w_down, g_next)


def _pad_ff(a):
    gate, val = jnp.split(a, 2, axis=-1)
    pad = [(0, 0)] * (a.ndim - 1) + [(0, D_FF_PAD - D_FF)]
    return jnp.concatenate([jnp.pad(gate, pad), jnp.pad(val, pad)], axis=-1)


def kernel(x, norm_mix, w_in, gm_ln_g, gm_ln_b, gm_w_s, gm_b_s, na_rpb, w_branch, w_out,
           norm_ffn, w_up, conv_k, conv_b, w_down, norm_final):
    b, t, d = x.shape
    assert (b, t, d) == (1, SEQ, D_MODEL)
    xs = x.reshape(t, d)
    bias, h = _prologue(na_rpb, xs, norm_mix[0].reshape(1, d))
    conv_k_p, conv_b_p = _pad_ff(conv_k), _pad_ff(conv_b.reshape(DEPTH, 1, -1))
    w_s_b = gm_w_s.astype(_BF16)
    uv0, qkv0, gt0 = 0, 2 * GM_WIDTH, 2 * GM_WIDTH + 3 * NA_WIDTH
    n_uv = (2 * GM_WIDTH // PROJ_TN) * (t // PROJ_TM)
    n_gt = (2 * D_MODEL // PROJ_TN) * (t // PROJ_TM)
    uv_sides = (
        SideCast(_cast_down_rows, w_down, D_FF_PAD // n_uv, D_FF_PAD, d),
        SideCast(_cast_rows, w_branch, w_branch.shape[1] // n_uv, w_branch.shape[1], d),
        SideCast(_cast_rows, w_out, d // n_uv, d, d),
    )
    gt_sides = (SideCast(_cast_up_rows, w_up, d // n_gt, d, 2 * D_FF_PAD),)
    for l in range(DEPTH):
        uv, w_down_b, w_branch_b, w_out_b = _proj(h, w_in, l, uv0, 2 * GM_WIDTH, "gelu",
                                                  "proj_uv", uv_sides)
        qkv, = _proj(h, w_in, l, qkv0, 3 * NA_WIDTH, "none", "proj_qkv", tm=PROJ_TM_PLAIN)
        gates, w_up_b = _proj(h, w_in, l, gt0, 2 * D_MODEL, "sigmoid", "proj_gates", gt_sides)
        yb = _na(qkv, bias, l)
        xs, h = _merge(xs, uv, gm_ln_g[l].reshape(1, -1), gm_ln_b[l].reshape(1, -1), w_s_b[l],
                       gm_b_s[l].reshape(GM_GROUPS, CHUNK, 1), yb, gates, w_branch_b, w_out_b,
                       norm_ffn[l].reshape(1, d))
        if l == DEPTH - 1:
            xs, = _ffn(xs, h, w_up_b, conv_k_p, conv_b_p, w_down_b, l,
                       norm_final.reshape(1, d), True)
        else:
            xs, h = _ffn(xs, h, w_up_b, conv_k_p, conv_b_p, w_down_b, l,
                         norm_mix[l + 1].reshape(1, d), False)
    return xs.reshape(b, t, d)
```

```python
from functools import partial
from typing import Callable, NamedTuple

import jax
import jax.numpy as jnp
from jax import lax
from jax.experimental import pallas as pl
from jax.experimental.pallas import tpu as pltpu

D_MODEL = 2048
SEQ = 8192
DEPTH = 2
GRID_W = 64
ROWS = SEQ // GRID_W
GM_GROUPS = 8
GM_HEAD = 128
GM_WIDTH = GM_GROUPS * GM_HEAD
CHUNK = 128
NA_HEADS = 16
NA_HEAD_DIM = 64
NA_WIDTH = NA_HEADS * NA_HEAD_DIM
NA_KR = 8
NA_KC = 16
D_FF = 5504
RMS_EPS = 1e-6
LN_EPS = 1e-5
NEG_INF = -1e30

LANES = 128
BF16_SUBLANES = 16
VMEM_BYTES = 64 * 1024 * 1024
VMEM_LIMIT_BYTES = VMEM_BYTES - 4 * 1024 * 1024

FF_TILE = 512
D_FF_PAD = ((D_FF + FF_TILE - 1) // FF_TILE) * FF_TILE
N_FF_TILES = D_FF_PAD // FF_TILE
PROJ_TM = 1024
PROJ_TM_PLAIN = 2048
PROJ_TN = 1024
MERGE_TM = 512
FFN_TM = 1024
FFN_LAG = 0
HALO = BF16_SUBLANES
NA_KTOK = NA_KR * GRID_W
NA_ROWS_PER_STEP = 64
NA_PV_LAG = 3
NA_CLASS_ROW = (0, 1, 2, 3, ROWS // 2, ROWS - 3, ROWS - 2, ROWS - 1)
N_CLASSES = len(NA_CLASS_ROW)
NA_INTERIOR_CLASS = NA_KR // 2

_BF16 = jnp.bfloat16
_F32 = jnp.float32


def _params(sem):
    return pltpu.CompilerParams(dimension_semantics=sem, vmem_limit_bytes=VMEM_LIMIT_BYTES)


def _rms(x, g):
    return x * lax.rsqrt(jnp.mean(x * x, axis=-1, keepdims=True) + RMS_EPS) * g


def _gelu_tanh(x):
    return 0.5 * x * (1.0 + jnp.tanh(0.7978845608028654 * (x + 0.044715 * (x * x * x))))


def _sigmoid(x):
    return 1.0 / (1.0 + jnp.exp(-x))


_ACTS = {"gelu": _gelu_tanh, "sigmoid": _sigmoid, "none": lambda x: x}


class SideCast(NamedTuple):
    body: Callable
    array: jax.Array
    rows: int
    out_rows: int
    out_cols: int


def _cast_rows(x_ref, o_ref, step):
    del step
    o_ref[...] = x_ref[...].astype(o_ref.dtype)


def _cast_up_rows(x_ref, o_ref, step):
    del step
    zeros = jnp.zeros((x_ref.shape[0], D_FF_PAD - D_FF), o_ref.dtype)
    o_ref[:, :D_FF] = x_ref[:, :D_FF].astype(o_ref.dtype)
    o_ref[:, D_FF:D_FF_PAD] = zeros
    o_ref[:, D_FF_PAD:D_FF_PAD + D_FF] = x_ref[:, D_FF:].astype(o_ref.dtype)
    o_ref[:, D_FF_PAD + D_FF:] = zeros


def _cast_down_rows(x_ref, o_ref, step):
    row = step * x_ref.shape[0] + lax.broadcasted_iota(jnp.int32, x_ref.shape, 0)
    o_ref[...] = jnp.where(row < D_FF, x_ref[...], 0.0).astype(o_ref.dtype)


def _proj_kernel(*refs, act, side_bodies):
    n = len(side_bodies)
    h_ref, w_ref = refs[:2]
    side_in = refs[2:2 + n]
    o_ref = refs[2 + n]
    side_out = refs[3 + n:3 + 2 * n]
    w_scr = refs[3 + 2 * n]

    @pl.when(pl.program_id(1) == 0)
    def _():
        w_scr[...] = w_ref[...].astype(_BF16)

    step = pl.program_id(0) * pl.num_programs(1) + pl.program_id(1)
    for body, x_ref, y_ref in zip(side_bodies, side_in, side_out):
        body(x_ref, y_ref, step)

    acc = jnp.dot(h_ref[...], w_scr[...], preferred_element_type=_F32)
    o_ref[...] = _ACTS[act](acc).astype(o_ref.dtype)


def _proj(h, w_in, layer, col0, ncols, act, name, sides=(), tm=PROJ_TM):
    m = h.shape[0]
    jb0 = col0 // PROJ_TN
    n_i = m // tm
    grid = (ncols // PROJ_TN, n_i)
    slab = lambda j, i: (j * n_i + i, 0)
    for s in sides:
        assert s.rows * grid[0] * grid[1] == s.out_rows, (name, s.rows, s.out_rows)
    outs = pl.pallas_call(
        partial(_proj_kernel, act=act, side_bodies=tuple(s.body for s in sides)),
        grid=grid,
        in_specs=[
            pl.BlockSpec((tm, D_MODEL), lambda j, i: (i, 0)),
            pl.BlockSpec((None, D_MODEL, PROJ_TN), lambda j, i: (layer, 0, jb0 + j)),
        ] + [pl.BlockSpec((None, s.rows, s.array.shape[2]), lambda j, i: (layer, j * n_i + i, 0))
             for s in sides],
        out_specs=[pl.BlockSpec((tm, PROJ_TN), lambda j, i: (i, j))]
        + [pl.BlockSpec((s.rows, s.out_cols), slab) for s in sides],
        out_shape=[jax.ShapeDtypeStruct((m, ncols), _BF16)]
        + [jax.ShapeDtypeStruct((s.out_rows, s.out_cols), _BF16) for s in sides],
        scratch_shapes=[pltpu.VMEM((D_MODEL, PROJ_TN), _BF16)],
        compiler_params=_params(("arbitrary", "arbitrary")),
        name=name,
    )(h, w_in, *[s.array for s in sides])
    return outs


def _gmlp_gate(uv_ref, lng_ref, lnb_ref, ws_ref, bs_ref, ya_ref):
    v = uv_ref[:, GM_WIDTH:].astype(_F32)
    mu = jnp.mean(v, axis=-1, keepdims=True)
    vc = v - mu
    var = jnp.mean(vc * vc, axis=-1, keepdims=True)
    vn = (vc * lax.rsqrt(var + LN_EPS) * lng_ref[...] + lnb_ref[...]).astype(_BF16)
    for n in range(uv_ref.shape[0] // CHUNK):
        rows = slice(n * CHUNK, (n + 1) * CHUNK)
        for g in range(GM_GROUPS):
            cols = slice(g * GM_HEAD, (g + 1) * GM_HEAD)
            mixed = jnp.dot(ws_ref[g], vn[rows, cols], preferred_element_type=_F32) + bs_ref[g]
            ya_ref[rows, cols] = (uv_ref[rows, cols].astype(_F32) * mixed).astype(ya_ref.dtype)


RPB_LANE0 = LANES // 2 - (NA_KC - 1)


def _prologue_kernel(rpb_ref, x_ref, g_ref, b_ref, h_ref):
    h_ref[...] = _rms(x_ref[...], g_ref[...]).astype(h_ref.dtype)

    c = lax.broadcasted_iota(jnp.int32, (GRID_W, LANES), 0)
    lane = lax.broadcasted_iota(jnp.int32, (GRID_W, LANES), 1)
    first = lane < GRID_W
    k = jnp.where(first, lane, lane - GRID_W)
    cs = jnp.clip(c - NA_KC // 2, 0, GRID_W - NA_KC)
    in_win = (k >= cs) & (k < cs + NA_KC)

    def toeplitz(dr, shift):
        row = jnp.broadcast_to(rpb_ref[0, dr:dr + 1, :], (GRID_W, LANES))
        return pltpu.roll(row, shift, axis=1, stride=1, stride_axis=0)

    n_dr = 2 * NA_KR - 1
    pairs = [jnp.where(in_win, jnp.where(first, toeplitz(dr, LANES // 2), toeplitz(dr + 1, 0)),
                       NEG_INF) for dr in range(n_dr - 1)]
    for cls, r in enumerate(NA_CLASS_ROW):
        rs = min(max(r - NA_KR // 2, 0), ROWS - NA_KR)
        for i in range(0, NA_KR, 2):
            b_ref[0, cls, :, i * GRID_W:(i + 2) * GRID_W] = pairs[rs + i - r + NA_KR - 1]


def _prologue(rpb, x, g):
    n = DEPTH * NA_HEADS
    m = x.shape[0]
    rows = m // n
    rpb_p = jnp.pad(rpb.reshape(n, 2 * NA_KR - 1, 2 * NA_KC - 1),
                    ((0, 0), (0, 1), (RPB_LANE0, LANES - RPB_LANE0 - (2 * NA_KC - 1))))
    return pl.pallas_call(
        _prologue_kernel,
        grid=(n,),
        in_specs=[pl.BlockSpec((1, 2 * NA_KR, LANES), lambda s: (s, 0, 0)),
                  pl.BlockSpec((rows, D_MODEL), lambda s: (s, 0)),
                  pl.BlockSpec((1, D_MODEL), lambda s: (0, 0))],
        out_specs=[pl.BlockSpec((1, N_CLASSES, GRID_W, NA_KTOK), lambda s: (s // 2, 0, s % 2, 0)),
                   pl.BlockSpec((rows, D_MODEL), lambda s: (s, 0))],
        out_shape=[jax.ShapeDtypeStruct((n // 2, N_CLASSES, 2 * GRID_W, NA_KTOK), _F32),
                   jax.ShapeDtypeStruct((m, D_MODEL), _BF16)],
        compiler_params=_params(("arbitrary",)),
        name="prologue",
    )(rpb_p, x, g)


def _na_window(r):
    rs = jnp.clip(r - NA_KR // 2, 0, ROWS - NA_KR)
    cls = jnp.where(r < NA_INTERIOR_CLASS, r,
                    jnp.where(r > ROWS - NA_KR // 2, r - (ROWS - NA_KR), NA_INTERIOR_CLASS))
    return pl.ds(pl.multiple_of(rs * GRID_W, GRID_W), NA_KTOK), cls


def _na_kernel(q_ref, k_ref, v_ref, b_ref, o_ref):
    rb = pl.program_id(1)
    scale = NA_HEAD_DIM ** -0.5
    head0 = lax.broadcasted_iota(jnp.int32, (GRID_W, LANES), 1) < NA_HEAD_DIM

    def apply(t, p, inv):
        qrows = slice(t * GRID_W, (t + 1) * GRID_W)
        vrows, _ = _na_window(rb * NA_ROWS_PER_STEP + t)
        o = jnp.dot(p, v_ref[vrows, :], preferred_element_type=_F32) * inv
        o_ref[qrows, :] = jnp.where(head0, o[:GRID_W], o[GRID_W:]).astype(o_ref.dtype)

    pending = []
    for t in range(NA_ROWS_PER_STEP):
        krows, cls = _na_window(rb * NA_ROWS_PER_STEP + t)
        q = q_ref[t * GRID_W:(t + 1) * GRID_W, :] * scale
        zero = jnp.zeros_like(q)
        q2 = jnp.concatenate([jnp.where(head0, q, zero), jnp.where(head0, zero, q)], axis=0)
        s = lax.dot_general(q2, k_ref[krows, :], (((1,), (1,)), ((), ())),
                            preferred_element_type=_F32) + b_ref[0, cls]
        p = jnp.exp(s - jnp.max(s, axis=-1, keepdims=True))
        den = jnp.sum(p, axis=-1, keepdims=True)
        pending.append((t, p.astype(_BF16), 1.0 / den))
        if len(pending) > NA_PV_LAG:
            apply(*pending.pop(0))
    for item in pending:
        apply(*item)


def _na(qkv, bias, layer):
    m = qkv.shape[0]
    tq = NA_ROWS_PER_STEP * GRID_W
    n_hp = NA_WIDTH // LANES
    return pl.pallas_call(
        _na_kernel,
        grid=(n_hp, m // tq),
        in_specs=[
            pl.BlockSpec((tq, LANES), lambda hp, rb: (rb, hp)),
            pl.BlockSpec((m, LANES), lambda hp, rb: (0, n_hp + hp)),
            pl.BlockSpec((m, LANES), lambda hp, rb: (0, 2 * n_hp + hp)),
            pl.BlockSpec((1, N_CLASSES, 2 * GRID_W, NA_KTOK),
                         lambda hp, rb: (layer * n_hp + hp, 0, 0, 0)),
        ],
        out_specs=pl.BlockSpec((tq, LANES), lambda hp, rb: (rb, hp)),
        out_shape=jax.ShapeDtypeStruct((m, NA_WIDTH), _BF16),
        compiler_params=_params(("arbitrary", "arbitrary")),
        name="na_attn",
    )(qkv, qkv, qkv, bias)


def _merge_kernel(x_ref, uv_ref, lng_ref, lnb_ref, ws_ref, bs_ref, yb_ref, gt_ref, wb_ref, wo_ref,
                  g_ref, o_ref, h_ref, ya_scr):
    _gmlp_gate(uv_ref, lng_ref, lnb_ref, ws_ref, bs_ref, ya_scr)
    ta = jnp.dot(ya_scr[...], wb_ref[:GM_WIDTH, :], preferred_element_type=_F32)
    tb = jnp.dot(yb_ref[...], wb_ref[GM_WIDTH:, :], preferred_element_type=_F32)
    merged = (gt_ref[:, :D_MODEL].astype(_F32) * ta + gt_ref[:, D_MODEL:].astype(_F32) * tb)
    y = x_ref[...] + jnp.dot(merged.astype(_BF16), wo_ref[...], preferred_element_type=_F32)
    o_ref[...] = y
    h_ref[...] = _rms(y, g_ref[...]).astype(h_ref.dtype)


def _merge(x, uv, ln_g, ln_b, w_s, b_s, yb, gates, w_branch, w_out, g_ffn):
    m = x.shape[0]
    resident = dict(pipeline_mode=pl.Buffered(1))
    row = lambda i: (i, 0)
    fixed2 = lambda i: (0, 0)
    fixed3 = lambda i: (0, 0, 0)
    return pl.pallas_call(
        _merge_kernel,
        grid=(m // MERGE_TM,),
        in_specs=[
            pl.BlockSpec((MERGE_TM, D_MODEL), row),
            pl.BlockSpec((MERGE_TM, 2 * GM_WIDTH), row),
            pl.BlockSpec((1, GM_WIDTH), fixed2),
            pl.BlockSpec((1, GM_WIDTH), fixed2),
            pl.BlockSpec((GM_GROUPS, CHUNK, CHUNK), fixed3),
            pl.BlockSpec((GM_GROUPS, CHUNK, 1), fixed3),
            pl.BlockSpec((MERGE_TM, NA_WIDTH), row),
            pl.BlockSpec((MERGE_TM, 2 * D_MODEL), row),
            pl.BlockSpec((GM_WIDTH + NA_WIDTH, D_MODEL), fixed2, **resident),
            pl.BlockSpec((D_MODEL, D_MODEL), fixed2, **resident),
            pl.BlockSpec((1, D_MODEL), fixed2),
        ],
        out_specs=[pl.BlockSpec((MERGE_TM, D_MODEL), row), pl.BlockSpec((MERGE_TM, D_MODEL), row)],
        out_shape=[jax.ShapeDtypeStruct((m, D_MODEL), _F32),
                   jax.ShapeDtypeStruct((m, D_MODEL), _BF16)],
        scratch_shapes=[pltpu.VMEM((MERGE_TM, GM_WIDTH), _BF16)],
        compiler_params=_params(("arbitrary",)),
        name="merge_out",
    )(x, uv, ln_g, ln_b, w_s, b_s, yb, gates, w_branch, w_out, g_ffn)


def _ffn_kernel(x_hbm, h_ref, hp_ref, hn_ref, wg_ref, wv_ref, kg_ref, kv_ref, bg_ref, bv_ref,
                wd_ref, gn_ref, *refs, final_norm):
    if final_norm:
        o_ref, h_scr, ug_scr, uv_scr, act_scr, x_sem = refs
    else:
        o_ref, ho_ref, h_scr, ug_scr, uv_scr, act_scr, x_sem = refs
    i, j = pl.program_id(0), pl.program_id(1)
    tm = FFN_TM

    def x_copy():
        rows = pl.ds(pl.multiple_of(i * tm, tm), tm)
        return pltpu.make_async_copy(x_hbm.at[rows, :], o_ref, x_sem)

    def up_dots():
        h = h_scr[...]
        for u_scr, w_ref in ((ug_scr, wg_ref), (uv_scr, wv_ref)):
            u = jnp.dot(h, w_ref[...], preferred_element_type=_F32)
            for c in range(FF_TILE // LANES):
                u_scr[c] = u[:, c * LANES:(c + 1) * LANES]

    def conv(u_scr, k_ref, b_ref, c):
        cols = slice(c * LANES, (c + 1) * LANES)
        return (u_scr[c, HALO - 1:HALO - 1 + tm, :] * k_ref[0:1, cols]
                + u_scr[c, HALO:HALO + tm, :] * k_ref[1:2, cols]
                + u_scr[c, HALO + 1:HALO + 1 + tm, :] * k_ref[2:3, cols] + b_ref[:, cols])

    def gate_into(slot):
        for c in range(FF_TILE // LANES):
            gate = conv(ug_scr, kg_ref, bg_ref, c)
            val = conv(uv_scr, kv_ref, bv_ref, c)
            act_scr[slot, :, c * LANES:(c + 1) * LANES] = (
                gate * _sigmoid(gate) * val).astype(_BF16)

    def down_from(slot):
        o_ref[...] += jnp.dot(act_scr[slot], wd_ref[...], preferred_element_type=_F32)

    @pl.when(j == 0)
    def _():
        zero = jnp.zeros((HALO, D_MODEL), _BF16)
        h_scr[:HALO, :] = jnp.where(i > 0, hp_ref[...], zero)
        h_scr[HALO:HALO + tm, :] = h_ref[...]
        h_scr[HALO + tm:, :] = jnp.where(i < pl.num_programs(0) - 1, hn_ref[...], zero)

    def epilogue():
        hn = _rms(o_ref[...], gn_ref[...])
        if final_norm:
            o_ref[...] = hn
        else:
            ho_ref[...] = hn.astype(ho_ref.dtype)

    if FFN_LAG == 0:
        @pl.when(j == 0)
        def _():
            x_copy().start()
            up_dots()
            gate_into(0)
            x_copy().wait()
            down_from(0)

        @pl.when(j > 0)
        def _():
            up_dots()
            gate_into(0)
            down_from(0)

        pl.when(j == N_FF_TILES - 1)(epilogue)
        return

    slots = FFN_LAG + 1
    for s in range(FFN_LAG):
        @pl.when(j == s)
        def _(s=s):
            if s == FFN_LAG - 1:
                x_copy().start()
            up_dots()
            gate_into(s)

    @pl.when(j == FFN_LAG)
    def _():
        x_copy().wait()

    for r in range(slots):
        @pl.when((j >= FFN_LAG) & (j < N_FF_TILES) & (j % slots == r))
        def _(r=r):
            up_dots()
            down_from((r - FFN_LAG) % slots)
            gate_into(r)

    for s in range(N_FF_TILES, N_FF_TILES + FFN_LAG):
        @pl.when(j == s)
        def _(s=s):
            down_from((s - FFN_LAG) % slots)
            if s == N_FF_TILES + FFN_LAG - 1:
                hn = _rms(o_ref[...], gn_ref[...])
                if final_norm:
                    o_ref[...] = hn
                else:
                    ho_ref[...] = hn.astype(ho_ref.dtype)


def _ffn(x, h, w_up, conv_k, conv_b, w_down, layer, g_next, final_norm):
    m = x.shape[0]
    tm = FFN_TM
    nb = tm // HALO
    last = m // HALO - 1
    row = lambda i, j: (i, 0)
    up_tile = lambda j: jnp.minimum(j, N_FF_TILES - 1)
    gate = lambda i, j: (layer, 0, up_tile(j))
    val = lambda i, j: (layer, 0, N_FF_TILES + up_tile(j))
    out_specs = [pl.BlockSpec((tm, D_MODEL), row)]
    out_shape = [jax.ShapeDtypeStruct((m, D_MODEL), _F32)]
    if not final_norm:
        out_specs.append(pl.BlockSpec((tm, D_MODEL), row))
        out_shape.append(jax.ShapeDtypeStruct((m, D_MODEL), _BF16))
    return pl.pallas_call(
        partial(_ffn_kernel, final_norm=final_norm),
        grid=(m // tm, N_FF_TILES + FFN_LAG),
        in_specs=[
            pl.BlockSpec(memory_space=pl.ANY),
            pl.BlockSpec((tm, D_MODEL), row),
            pl.BlockSpec((HALO, D_MODEL), lambda i, j: (jnp.maximum(i * nb - 1, 0), 0)),
            pl.BlockSpec((HALO, D_MODEL), lambda i, j: (jnp.minimum((i + 1) * nb, last), 0)),
            pl.BlockSpec((D_MODEL, FF_TILE), lambda i, j: (0, up_tile(j))),
            pl.BlockSpec((D_MODEL, FF_TILE), lambda i, j: (0, N_FF_TILES + up_tile(j))),
            pl.BlockSpec((None, 3, FF_TILE), gate),
            pl.BlockSpec((None, 3, FF_TILE), val),
            pl.BlockSpec((None, 1, FF_TILE), gate),
            pl.BlockSpec((None, 1, FF_TILE), val),
            pl.BlockSpec((FF_TILE, D_MODEL),
                         lambda i, j: (jnp.clip(j - FFN_LAG, 0, N_FF_TILES - 1), 0)),
            pl.BlockSpec((1, D_MODEL), lambda i, j: (0, 0)),
        ],
        out_specs=out_specs,
        out_shape=out_shape,
        scratch_shapes=[
            pltpu.VMEM((tm + 2 * HALO, D_MODEL), _BF16),
            pltpu.VMEM((FF_TILE // LANES, tm + 2 * HALO, LANES), _F32),
            pltpu.VMEM((FF_TILE // LANES, tm + 2 * HALO, LANES), _F32),
            pltpu.VMEM((FFN_LAG + 1, tm, FF_TILE), _BF16),
            pltpu.SemaphoreType.DMA(()),
        ],
        compiler_params=_params(("arbitrary", "arbitrary")),
        name="conv_ffn",
    )(x, h, h, h, w_up, w_up, conv_k, conv_k, conv_b, conv_b, w_down, g_next)


def _pad_ff(a):
    gate, val = jnp.split(a, 2, axis=-1)
    pad = [(0, 0)] * (a.ndim - 1) + [(0, D_FF_PAD - D_FF)]
    return jnp.concatenate([jnp.pad(gate, pad), jnp.pad(val, pad)], axis=-1)


def kernel(x, norm_mix, w_in, gm_ln_g, gm_ln_b, gm_w_s, gm_b_s, na_rpb, w_branch, w_out,
           norm_ffn, w_up, conv_k, conv_b, w_down, norm_final):
    b, t, d = x.shape
    assert (b, t, d) == (1, SEQ, D_MODEL)
    xs = x.reshape(t, d)
    bias, h = _prologue(na_rpb, xs, norm_mix[0].reshape(1, d))
    conv_k_p, conv_b_p = _pad_ff(conv_k), _pad_ff(conv_b.reshape(DEPTH, 1, -1))
    w_s_b = gm_w_s.astype(_BF16)
    uv0, qkv0, gt0 = 0, 2 * GM_WIDTH, 2 * GM_WIDTH + 3 * NA_WIDTH
    n_uv = (2 * GM_WIDTH // PROJ_TN) * (t // PROJ_TM)
    n_gt = (2 * D_MODEL // PROJ_TN) * (t // PROJ_TM)
    uv_sides = (
        SideCast(_cast_down_rows, w_down, D_FF_PAD // n_uv, D_FF_PAD, d),
        SideCast(_cast_rows, w_branch, w_branch.shape[1] // n_uv, w_branch.shape[1], d),
        SideCast(_cast_rows, w_out, d // n_uv, d, d),
    )
    gt_sides = (SideCast(_cast_up_rows, w_up, d // n_gt, d, 2 * D_FF_PAD),)
    for l in range(DEPTH):
        uv, w_down_b, w_branch_b, w_out_b = _proj(h, w_in, l, uv0, 2 * GM_WIDTH, "gelu",
                                                  "proj_uv", uv_sides)
        qkv, = _proj(h, w_in, l, qkv0, 3 * NA_WIDTH, "none", "proj_qkv", tm=PROJ_TM_PLAIN)
        gates, w_up_b = _proj(h, w_in, l, gt0, 2 * D_MODEL, "sigmoid", "proj_gates", gt_sides)
        yb = _na(qkv, bias, l)
        xs, h = _merge(xs, uv, gm_ln_g[l].reshape(1, -1), gm_ln_b[l].reshape(1, -1), w_s_b[l],
                       gm_b_s[l].reshape(GM_GROUPS, CHUNK, 1), yb, gates, w_branch_b, w_out_b,
                       norm_ffn[l].reshape(1, d))
        if l == DEPTH - 1:
            xs, = _ffn(xs, h, w_up_b, conv_k_p, conv_b_p, w_down_b, l,
                       norm_final.reshape(1, d), True)
        else:
            xs, h = _ffn(xs, h, w_up_b, conv_k_p, conv_b_p, w_down_b, l,
                         norm_mix[l + 1].reshape(1, d), False)
    return xs.reshape(b, t, d)
```

```python
from functools import partial
from typing import Callable, NamedTuple

import jax
import jax.numpy as jnp
from jax import lax
from jax.experimental import pallas as pl
from jax.experimental.pallas import tpu as pltpu

D_MODEL = 2048
SEQ = 8192
DEPTH = 2
GRID_W = 64
ROWS = SEQ // GRID_W
GM_GROUPS = 8
GM_HEAD = 128
GM_WIDTH = GM_GROUPS * GM_HEAD
CHUNK = 128
NA_HEADS = 16
NA_HEAD_DIM = 64
NA_WIDTH = NA_HEADS * NA_HEAD_DIM
NA_KR = 8
NA_KC = 16
D_FF = 5504
RMS_EPS = 1e-6
LN_EPS = 1e-5
NEG_INF = -1e30

LANES = 128
BF16_SUBLANES = 16
VMEM_BYTES = 64 * 1024 * 1024
VMEM_LIMIT_BYTES = VMEM_BYTES - 4 * 1024 * 1024

FF_TILE = 512
D_FF_PAD = ((D_FF + FF_TILE - 1) // FF_TILE) * FF_TILE
N_FF_TILES = D_FF_PAD // FF_TILE
PROJ_TM = 1024
PROJ_TM_PLAIN = 2048
PROJ_TN = 1024
MERGE_TM = 512
FFN_TM = 1024
FFN_LAG = 1
HALO = BF16_SUBLANES
NA_KTOK = NA_KR * GRID_W
NA_ROWS_PER_STEP = 128
NA_PV_LAG = 3
NA_CLASS_ROW = (0, 1, 2, 3, ROWS // 2, ROWS - 3, ROWS - 2, ROWS - 1)
N_CLASSES = len(NA_CLASS_ROW)
NA_INTERIOR_CLASS = NA_KR // 2

_BF16 = jnp.bfloat16
_F32 = jnp.float32


def _params(sem):
    return pltpu.CompilerParams(dimension_semantics=sem, vmem_limit_bytes=VMEM_LIMIT_BYTES)


def _rms(x, g):
    return x * lax.rsqrt(jnp.mean(x * x, axis=-1, keepdims=True) + RMS_EPS) * g


def _gelu_tanh(x):
    return 0.5 * x * (1.0 + jnp.tanh(0.7978845608028654 * (x + 0.044715 * (x * x * x))))


def _sigmoid(x):
    return 1.0 / (1.0 + jnp.exp(-x))


_ACTS = {"gelu": _gelu_tanh, "sigmoid": _sigmoid, "none": lambda x: x}


class SideCast(NamedTuple):
    body: Callable
    array: jax.Array
    rows: int
    out_rows: int
    out_cols: int


def _cast_rows(x_ref, o_ref, step):
    del step
    o_ref[...] = x_ref[...].astype(o_ref.dtype)


def _cast_up_rows(x_ref, o_ref, step):
    del step
    zeros = jnp.zeros((x_ref.shape[0], D_FF_PAD - D_FF), o_ref.dtype)
    o_ref[:, :D_FF] = x_ref[:, :D_FF].astype(o_ref.dtype)
    o_ref[:, D_FF:D_FF_PAD] = zeros
    o_ref[:, D_FF_PAD:D_FF_PAD + D_FF] = x_ref[:, D_FF:].astype(o_ref.dtype)
    o_ref[:, D_FF_PAD + D_FF:] = zeros


def _cast_down_rows(x_ref, o_ref, step):
    row = step * x_ref.shape[0] + lax.broadcasted_iota(jnp.int32, x_ref.shape, 0)
    o_ref[...] = jnp.where(row < D_FF, x_ref[...], 0.0).astype(o_ref.dtype)


def _proj_kernel(*refs, act, side_bodies):
    n = len(side_bodies)
    h_ref, w_ref = refs[:2]
    side_in = refs[2:2 + n]
    o_ref = refs[2 + n]
    side_out = refs[3 + n:3 + 2 * n]
    w_scr = refs[3 + 2 * n]

    @pl.when(pl.program_id(1) == 0)
    def _():
        w_scr[...] = w_ref[...].astype(_BF16)

    step = pl.program_id(0) * pl.num_programs(1) + pl.program_id(1)
    for body, x_ref, y_ref in zip(side_bodies, side_in, side_out):
        body(x_ref, y_ref, step)

    acc = jnp.dot(h_ref[...], w_scr[...], preferred_element_type=_F32)
    o_ref[...] = _ACTS[act](acc).astype(o_ref.dtype)


def _proj(h, w_in, layer, col0, ncols, act, name, sides=(), tm=PROJ_TM):
    m = h.shape[0]
    jb0 = col0 // PROJ_TN
    n_i = m // tm
    grid = (ncols // PROJ_TN, n_i)
    slab = lambda j, i: (j * n_i + i, 0)
    for s in sides:
        assert s.rows * grid[0] * grid[1] == s.out_rows, (name, s.rows, s.out_rows)
    outs = pl.pallas_call(
        partial(_proj_kernel, act=act, side_bodies=tuple(s.body for s in sides)),
        grid=grid,
        in_specs=[
            pl.BlockSpec((tm, D_MODEL), lambda j, i: (i, 0)),
            pl.BlockSpec((None, D_MODEL, PROJ_TN), lambda j, i: (layer, 0, jb0 + j)),
        ] + [pl.BlockSpec((None, s.rows, s.array.shape[2]), lambda j, i: (layer, j * n_i + i, 0))
             for s in sides],
        out_specs=[pl.BlockSpec((tm, PROJ_TN), lambda j, i: (i, j))]
        + [pl.BlockSpec((s.rows, s.out_cols), slab) for s in sides],
        out_shape=[jax.ShapeDtypeStruct((m, ncols), _BF16)]
        + [jax.ShapeDtypeStruct((s.out_rows, s.out_cols), _BF16) for s in sides],
        scratch_shapes=[pltpu.VMEM((D_MODEL, PROJ_TN), _BF16)],
        compiler_params=_params(("arbitrary", "arbitrary")),
        name=name,
    )(h, w_in, *[s.array for s in sides])
    return outs


def _gmlp_gate(uv_ref, lng_ref, lnb_ref, ws_ref, bs_ref, ya_ref):
    v = uv_ref[:, GM_WIDTH:].astype(_F32)
    mu = jnp.mean(v, axis=-1, keepdims=True)
    vc = v - mu
    var = jnp.mean(vc * vc, axis=-1, keepdims=True)
    vn = (vc * lax.rsqrt(var + LN_EPS) * lng_ref[...] + lnb_ref[...]).astype(_BF16)
    for n in range(uv_ref.shape[0] // CHUNK):
        rows = slice(n * CHUNK, (n + 1) * CHUNK)
        for g in range(GM_GROUPS):
            cols = slice(g * GM_HEAD, (g + 1) * GM_HEAD)
            mixed = jnp.dot(ws_ref[g], vn[rows, cols], preferred_element_type=_F32) + bs_ref[g]
            ya_ref[rows, cols] = (uv_ref[rows, cols].astype(_F32) * mixed).astype(ya_ref.dtype)


RPB_LANE0 = LANES // 2 - (NA_KC - 1)


def _prologue_kernel(rpb_ref, x_ref, g_ref, b_ref, h_ref):
    h_ref[...] = _rms(x_ref[...], g_ref[...]).astype(h_ref.dtype)

    c = lax.broadcasted_iota(jnp.int32, (GRID_W, LANES), 0)
    lane = lax.broadcasted_iota(jnp.int32, (GRID_W, LANES), 1)
    first = lane < GRID_W
    k = jnp.where(first, lane, lane - GRID_W)
    cs = jnp.clip(c - NA_KC // 2, 0, GRID_W - NA_KC)
    in_win = (k >= cs) & (k < cs + NA_KC)

    def toeplitz(dr, shift):
        row = jnp.broadcast_to(rpb_ref[0, dr:dr + 1, :], (GRID_W, LANES))
        return pltpu.roll(row, shift, axis=1, stride=1, stride_axis=0)

    n_dr = 2 * NA_KR - 1
    pairs = [jnp.where(in_win, jnp.where(first, toeplitz(dr, LANES // 2), toeplitz(dr + 1, 0)),
                       NEG_INF) for dr in range(n_dr - 1)]
    for cls, r in enumerate(NA_CLASS_ROW):
        rs = min(max(r - NA_KR // 2, 0), ROWS - NA_KR)
        for i in range(0, NA_KR, 2):
            b_ref[0, cls, :, i * GRID_W:(i + 2) * GRID_W] = pairs[rs + i - r + NA_KR - 1]


def _prologue(rpb, x, g):
    n = DEPTH * NA_HEADS
    m = x.shape[0]
    rows = m // n
    rpb_p = jnp.pad(rpb.reshape(n, 2 * NA_KR - 1, 2 * NA_KC - 1),
                    ((0, 0), (0, 1), (RPB_LANE0, LANES - RPB_LANE0 - (2 * NA_KC - 1))))
    return pl.pallas_call(
        _prologue_kernel,
        grid=(n,),
        in_specs=[pl.BlockSpec((1, 2 * NA_KR, LANES), lambda s: (s, 0, 0)),
                  pl.BlockSpec((rows, D_MODEL), lambda s: (s, 0)),
                  pl.BlockSpec((1, D_MODEL), lambda s: (0, 0))],
        out_specs=[pl.BlockSpec((1, N_CLASSES, GRID_W, NA_KTOK), lambda s: (s // 2, 0, s % 2, 0)),
                   pl.BlockSpec((rows, D_MODEL), lambda s: (s, 0))],
        out_shape=[jax.ShapeDtypeStruct((n // 2, N_CLASSES, 2 * GRID_W, NA_KTOK), _F32),
                   jax.ShapeDtypeStruct((m, D_MODEL), _BF16)],
        compiler_params=_params(("arbitrary",)),
        name="prologue",
    )(rpb_p, x, g)


def _na_window(r):
    rs = jnp.clip(r - NA_KR // 2, 0, ROWS - NA_KR)
    cls = jnp.where(r < NA_INTERIOR_CLASS, r,
                    jnp.where(r > ROWS - NA_KR // 2, r - (ROWS - NA_KR), NA_INTERIOR_CLASS))
    return pl.ds(pl.multiple_of(rs * GRID_W, GRID_W), NA_KTOK), cls


def _na_kernel(q_ref, k_ref, v_ref, b_ref, o_ref):
    rb = pl.program_id(1)
    scale = NA_HEAD_DIM ** -0.5
    head0 = lax.broadcasted_iota(jnp.int32, (GRID_W, LANES), 1) < NA_HEAD_DIM

    def apply(t, p, inv):
        qrows = slice(t * GRID_W, (t + 1) * GRID_W)
        vrows, _ = _na_window(rb * NA_ROWS_PER_STEP + t)
        o = jnp.dot(p, v_ref[vrows, :], preferred_element_type=_F32) * inv
        o_ref[qrows, :] = jnp.where(head0, o[:GRID_W], o[GRID_W:]).astype(o_ref.dtype)

    pending = []
    for t in range(NA_ROWS_PER_STEP):
        krows, cls = _na_window(rb * NA_ROWS_PER_STEP + t)
        q = q_ref[t * GRID_W:(t + 1) * GRID_W, :] * scale
        zero = jnp.zeros_like(q)
        q2 = jnp.concatenate([jnp.where(head0, q, zero), jnp.where(head0, zero, q)], axis=0)
        s = lax.dot_general(q2, k_ref[krows, :], (((1,), (1,)), ((), ())),
                            preferred_element_type=_F32) + b_ref[0, cls]
        p = jnp.exp(s - jnp.max(s, axis=-1, keepdims=True))
        den = jnp.sum(p, axis=-1, keepdims=True)
        pending.append((t, p.astype(_BF16), 1.0 / den))
        if len(pending) > NA_PV_LAG:
            apply(*pending.pop(0))
    for item in pending:
        apply(*item)


def _na(qkv, bias, layer):
    m = qkv.shape[0]
    tq = NA_ROWS_PER_STEP * GRID_W
    n_hp = NA_WIDTH // LANES
    return pl.pallas_call(
        _na_kernel,
        grid=(n_hp, m // tq),
        in_specs=[
            pl.BlockSpec((tq, LANES), lambda hp, rb: (rb, hp)),
            pl.BlockSpec((m, LANES), lambda hp, rb: (0, n_hp + hp)),
            pl.BlockSpec((m, LANES), lambda hp, rb: (0, 2 * n_hp + hp)),
            pl.BlockSpec((1, N_CLASSES, 2 * GRID_W, NA_KTOK),
                         lambda hp, rb: (layer * n_hp + hp, 0, 0, 0)),
        ],
        out_specs=pl.BlockSpec((tq, LANES), lambda hp, rb: (rb, hp)),
        out_shape=jax.ShapeDtypeStruct((m, NA_WIDTH), _BF16),
        compiler_params=_params(("arbitrary", "arbitrary")),
        name="na_attn",
    )(qkv, qkv, qkv, bias)


def _merge_kernel(x_ref, uv_ref, lng_ref, lnb_ref, ws_ref, bs_ref, yb_ref, gt_ref, wb_ref, wo_ref,
                  g_ref, o_ref, h_ref, ya_scr):
    _gmlp_gate(uv_ref, lng_ref, lnb_ref, ws_ref, bs_ref, ya_scr)
    ta = jnp.dot(ya_scr[...], wb_ref[:GM_WIDTH, :], preferred_element_type=_F32)
    tb = jnp.dot(yb_ref[...], wb_ref[GM_WIDTH:, :], preferred_element_type=_F32)
    merged = (gt_ref[:, :D_MODEL].astype(_F32) * ta + gt_ref[:, D_MODEL:].astype(_F32) * tb)
    y = x_ref[...] + jnp.dot(merged.astype(_BF16), wo_ref[...], preferred_element_type=_F32)
    o_ref[...] = y
    h_ref[...] = _rms(y, g_ref[...]).astype(h_ref.dtype)


def _merge(x, uv, ln_g, ln_b, w_s, b_s, yb, gates, w_branch, w_out, g_ffn):
    m = x.shape[0]
    resident = dict(pipeline_mode=pl.Buffered(1))
    row = lambda i: (i, 0)
    fixed2 = lambda i: (0, 0)
    fixed3 = lambda i: (0, 0, 0)
    return pl.pallas_call(
        _merge_kernel,
        grid=(m // MERGE_TM,),
        in_specs=[
            pl.BlockSpec((MERGE_TM, D_MODEL), row),
            pl.BlockSpec((MERGE_TM, 2 * GM_WIDTH), row),
            pl.BlockSpec((1, GM_WIDTH), fixed2),
            pl.BlockSpec((1, GM_WIDTH), fixed2),
            pl.BlockSpec((GM_GROUPS, CHUNK, CHUNK), fixed3),
            pl.BlockSpec((GM_GROUPS, CHUNK, 1), fixed3),
            pl.BlockSpec((MERGE_TM, NA_WIDTH), row),
            pl.BlockSpec((MERGE_TM, 2 * D_MODEL), row),
            pl.BlockSpec((GM_WIDTH + NA_WIDTH, D_MODEL), fixed2, **resident),
            pl.BlockSpec((D_MODEL, D_MODEL), fixed2, **resident),
            pl.BlockSpec((1, D_MODEL), fixed2),
        ],
        out_specs=[pl.BlockSpec((MERGE_TM, D_MODEL), row), pl.BlockSpec((MERGE_TM, D_MODEL), row)],
        out_shape=[jax.ShapeDtypeStruct((m, D_MODEL), _F32),
                   jax.ShapeDtypeStruct((m, D_MODEL), _BF16)],
        scratch_shapes=[pltpu.VMEM((MERGE_TM, GM_WIDTH), _BF16)],
        compiler_params=_params(("arbitrary",)),
        name="merge_out",
    )(x, uv, ln_g, ln_b, w_s, b_s, yb, gates, w_branch, w_out, g_ffn)


def _ffn_kernel(x_hbm, h_ref, hp_ref, hn_ref, wg_ref, wv_ref, kg_ref, kv_ref, bg_ref, bv_ref,
                wd_ref, gn_ref, *refs, final_norm):
    if final_norm:
        o_ref, h_scr, ug_scr, uv_scr, act_scr, x_sem = refs
    else:
        o_ref, ho_ref, h_scr, ug_scr, uv_scr, act_scr, x_sem = refs
    i, j = pl.program_id(0), pl.program_id(1)
    tm = FFN_TM

    def x_copy():
        rows = pl.ds(pl.multiple_of(i * tm, tm), tm)
        return pltpu.make_async_copy(x_hbm.at[rows, :], o_ref, x_sem)

    def up_dots():
        h = h_scr[...]
        for u_scr, w_ref in ((ug_scr, wg_ref), (uv_scr, wv_ref)):
            u = jnp.dot(h, w_ref[...], preferred_element_type=_F32)
            for c in range(FF_TILE // LANES):
                u_scr[c] = u[:, c * LANES:(c + 1) * LANES]

    def conv(u_scr, k_ref, b_ref, c):
        cols = slice(c * LANES, (c + 1) * LANES)
        return (u_scr[c, HALO - 1:HALO - 1 + tm, :] * k_ref[0:1, cols]
                + u_scr[c, HALO:HALO + tm, :] * k_ref[1:2, cols]
                + u_scr[c, HALO + 1:HALO + 1 + tm, :] * k_ref[2:3, cols] + b_ref[:, cols])

    def gate_into(slot):
        for c in range(FF_TILE // LANES):
            gate = conv(ug_scr, kg_ref, bg_ref, c)
            val = conv(uv_scr, kv_ref, bv_ref, c)
            act_scr[slot, :, c * LANES:(c + 1) * LANES] = (
                gate * _sigmoid(gate) * val).astype(_BF16)

    def down_from(slot):
        o_ref[...] += jnp.dot(act_scr[slot], wd_ref[...], preferred_element_type=_F32)

    @pl.when(j == 0)
    def _():
        zero = jnp.zeros((HALO, D_MODEL), _BF16)
        h_scr[:HALO, :] = jnp.where(i > 0, hp_ref[...], zero)
        h_scr[HALO:HALO + tm, :] = h_ref[...]
        h_scr[HALO + tm:, :] = jnp.where(i < pl.num_programs(0) - 1, hn_ref[...], zero)

    slots = FFN_LAG + 1
    for s in range(FFN_LAG):
        @pl.when(j == s)
        def _(s=s):
            if s == FFN_LAG - 1:
                x_copy().start()
            up_dots()
            gate_into(s)

    @pl.when(j == FFN_LAG)
    def _():
        x_copy().wait()

    for r in range(slots):
        @pl.when((j >= FFN_LAG) & (j < N_FF_TILES) & (j % slots == r))
        def _(r=r):
            up_dots()
            down_from((r - FFN_LAG) % slots)
            gate_into(r)

    for s in range(N_FF_TILES, N_FF_TILES + FFN_LAG):
        @pl.when(j == s)
        def _(s=s):
            down_from((s - FFN_LAG) % slots)
            if s == N_FF_TILES + FFN_LAG - 1:
                hn = _rms(o_ref[...], gn_ref[...])
                if final_norm:
                    o_ref[...] = hn
                else:
                    ho_ref[...] = hn.astype(ho_ref.dtype)


def _ffn(x, h, w_up, conv_k, conv_b, w_down, layer, g_next, final_norm):
    m = x.shape[0]
    tm = FFN_TM
    nb = tm // HALO
    last = m // HALO - 1
    row = lambda i, j: (i, 0)
    up_tile = lambda j: jnp.minimum(j, N_FF_TILES - 1)
    gate = lambda i, j: (layer, 0, up_tile(j))
    val = lambda i, j: (layer, 0, N_FF_TILES + up_tile(j))
    out_specs = [pl.BlockSpec((tm, D_MODEL), row)]
    out_shape = [jax.ShapeDtypeStruct((m, D_MODEL), _F32)]
    if not final_norm:
        out_specs.append(pl.BlockSpec((tm, D_MODEL), row))
        out_shape.append(jax.ShapeDtypeStruct((m, D_MODEL), _BF16))
    return pl.pallas_call(
        partial(_ffn_kernel, final_norm=final_norm),
        grid=(m // tm, N_FF_TILES + FFN_LAG),
        in_specs=[
            pl.BlockSpec(memory_space=pl.ANY),
            pl.BlockSpec((tm, D_MODEL), row),
            pl.BlockSpec((HALO, D_MODEL), lambda i, j: (jnp.maximum(i * nb - 1, 0), 0)),
            pl.BlockSpec((HALO, D_MODEL), lambda i, j: (jnp.minimum((i + 1) * nb, last), 0)),
            pl.BlockSpec((D_MODEL, FF_TILE), lambda i, j: (0, up_tile(j))),
            pl.BlockSpec((D_MODEL, FF_TILE), lambda i, j: (0, N_FF_TILES + up_tile(j))),
            pl.BlockSpec((None, 3, FF_TILE), gate),
            pl.BlockSpec((None, 3, FF_TILE), val),
            pl.BlockSpec((None, 1, FF_TILE), gate),
            pl.BlockSpec((None, 1, FF_TILE), val),
            pl.BlockSpec((FF_TILE, D_MODEL),
                         lambda i, j: (jnp.clip(j - FFN_LAG, 0, N_FF_TILES - 1), 0)),
            pl.BlockSpec((1, D_MODEL), lambda i, j: (0, 0)),
        ],
        out_specs=out_specs,
        out_shape=out_shape,
        scratch_shapes=[
            pltpu.VMEM((tm + 2 * HALO, D_MODEL), _BF16),
            pltpu.VMEM((FF_TILE // LANES, tm + 2 * HALO, LANES), _F32),
            pltpu.VMEM((FF_TILE // LANES, tm + 2 * HALO, LANES), _F32),
            pltpu.VMEM((FFN_LAG + 1, tm, FF_TILE), _BF16),
            pltpu.SemaphoreType.DMA(()),
        ],
        compiler_params=_params(("arbitrary", "arbitrary")),
        name="conv_ffn",
    )(x, h, h, h, w_up, w_up, conv_k, conv_k, conv_b, conv_b, w_down, g_next)


def _pad_ff(a):
    gate, val = jnp.split(a, 2, axis=-1)
    pad = [(0, 0)] * (a.ndim - 1) + [(0, D_FF_PAD - D_FF)]
    return jnp.concatenate([jnp.pad(gate, pad), jnp.pad(val, pad)], axis=-1)


def kernel(x, norm_mix, w_in, gm_ln_g, gm_ln_b, gm_w_s, gm_b_s, na_rpb, w_branch, w_out,
           norm_ffn, w_up, conv_k, conv_b, w_down, norm_final):
    b, t, d = x.shape
    assert (b, t, d) == (1, SEQ, D_MODEL)
    xs = x.reshape(t, d)
    bias, h = _prologue(na_rpb, xs, norm_mix[0].reshape(1, d))
    conv_k_p, conv_b_p = _pad_ff(conv_k), _pad_ff(conv_b.reshape(DEPTH, 1, -1))
    w_s_b = gm_w_s.astype(_BF16)
    uv0, qkv0, gt0 = 0, 2 * GM_WIDTH, 2 * GM_WIDTH + 3 * NA_WIDTH
    n_uv = (2 * GM_WIDTH // PROJ_TN) * (t // PROJ_TM)
    n_gt = (2 * D_MODEL // PROJ_TN) * (t // PROJ_TM)
    uv_sides = (
        SideCast(_cast_down_rows, w_down, D_FF_PAD // n_uv, D_FF_PAD, d),
        SideCast(_cast_rows, w_branch, w_branch.shape[1] // n_uv, w_branch.shape[1], d),
        SideCast(_cast_rows, w_out, d // n_uv, d, d),
    )
    gt_sides = (SideCast(_cast_up_rows, w_up, d // n_gt, d, 2 * D_FF_PAD),)
    for l in range(DEPTH):
        uv, w_down_b, w_branch_b, w_out_b = _proj(h, w_in, l, uv0, 2 * GM_WIDTH, "gelu",
                                                  "proj_uv", uv_sides)
        qkv, = _proj(h, w_in, l, qkv0, 3 * NA_WIDTH, "none", "proj_qkv", tm=PROJ_TM_PLAIN)
        gates, w_up_b = _proj(h, w_in, l, gt0, 2 * D_MODEL, "sigmoid", "proj_gates", gt_sides)
        yb = _na(qkv, bias, l)
        xs, h = _merge(xs, uv, gm_ln_g[l].reshape(1, -1), gm_ln_b[l].reshape(1, -1), w_s_b[l],
                       gm_b_s[l].reshape(GM_GROUPS, CHUNK, 1), yb, gates, w_branch_b, w_out_b,
                       norm_ffn[l].reshape(1, d))
        if l == DEPTH - 1:
            xs, = _ffn(xs, h, w_up_b, conv_k_p, conv_b_p, w_down_b, l,
                       norm_final.reshape(1, d), True)
        else:
            xs, h = _ffn(xs, h, w_up_b, conv_k_p, conv_b_p, w_down_b, l,
                         norm_mix[l + 1].reshape(1, d), False)
    return xs.reshape(b, t, d)
```

```python
from functools import partial
from typing import Callable, NamedTuple

import jax
import jax.numpy as jnp
from jax import lax
from jax.experimental import pallas as pl
from jax.experimental.pallas import tpu as pltpu

D_MODEL = 2048
SEQ = 8192
DEPTH = 2
GRID_W = 64
ROWS = SEQ // GRID_W
GM_GROUPS = 8
GM_HEAD = 128
GM_WIDTH = GM_GROUPS * GM_HEAD
CHUNK = 128
NA_HEADS = 16
NA_HEAD_DIM = 64
NA_WIDTH = NA_HEADS * NA_HEAD_DIM
NA_KR = 8
NA_KC = 16
D_FF = 5504
RMS_EPS = 1e-6
LN_EPS = 1e-5
NEG_INF = -1e30

LANES = 128
BF16_SUBLANES = 16
VMEM_BYTES = 64 * 1024 * 1024
VMEM_LIMIT_BYTES = VMEM_BYTES - 4 * 1024 * 1024

FF_TILE = 512
D_FF_PAD = ((D_FF + FF_TILE - 1) // FF_TILE) * FF_TILE
N_FF_TILES = D_FF_PAD // FF_TILE
PROJ_TM = 1024
PROJ_TM_PLAIN = 2048
PROJ_TN = 1024
MERGE_TM = 512
FFN_TM = 1024
FFN_LAG = 1
HALO = BF16_SUBLANES
NA_KTOK = NA_KR * GRID_W
NA_ROWS_PER_STEP = 64
NA_PV_LAG = 3
NA_CLASS_ROW = (0, 1, 2, 3, ROWS // 2, ROWS - 3, ROWS - 2, ROWS - 1)
N_CLASSES = len(NA_CLASS_ROW)
NA_INTERIOR_CLASS = NA_KR // 2

_BF16 = jnp.bfloat16
_F32 = jnp.float32


def _params(sem):
    return pltpu.CompilerParams(dimension_semantics=sem, vmem_limit_bytes=VMEM_LIMIT_BYTES)


def _rms(x, g):
    return x * lax.rsqrt(jnp.mean(x * x, axis=-1, keepdims=True) + RMS_EPS) * g


def _gelu_tanh(x):
    return 0.5 * x * (1.0 + jnp.tanh(0.7978845608028654 * (x + 0.044715 * (x * x * x))))


def _sigmoid(x):
    return 0.5 * jnp.tanh(0.5 * x) + 0.5


_ACTS = {"gelu": _gelu_tanh, "sigmoid": _sigmoid, "none": lambda x: x}


class SideCast(NamedTuple):
    body: Callable
    array: jax.Array
    rows: int
    out_rows: int
    out_cols: int


def _cast_rows(x_ref, o_ref, step):
    del step
    o_ref[...] = x_ref[...].astype(o_ref.dtype)


def _cast_up_rows(x_ref, o_ref, step):
    del step
    zeros = jnp.zeros((x_ref.shape[0], D_FF_PAD - D_FF), o_ref.dtype)
    o_ref[:, :D_FF] = x_ref[:, :D_FF].astype(o_ref.dtype)
    o_ref[:, D_FF:D_FF_PAD] = zeros
    o_ref[:, D_FF_PAD:D_FF_PAD + D_FF] = x_ref[:, D_FF:].astype(o_ref.dtype)
    o_ref[:, D_FF_PAD + D_FF:] = zeros


def _cast_down_rows(x_ref, o_ref, step):
    row = step * x_ref.shape[0] + lax.broadcasted_iota(jnp.int32, x_ref.shape, 0)
    o_ref[...] = jnp.where(row < D_FF, x_ref[...], 0.0).astype(o_ref.dtype)


def _proj_kernel(*refs, act, side_bodies):
    n = len(side_bodies)
    h_ref, w_ref = refs[:2]
    side_in = refs[2:2 + n]
    o_ref = refs[2 + n]
    side_out = refs[3 + n:3 + 2 * n]
    w_scr = refs[3 + 2 * n]

    @pl.when(pl.program_id(1) == 0)
    def _():
        w_scr[...] = w_ref[...].astype(_BF16)

    step = pl.program_id(0) * pl.num_programs(1) + pl.program_id(1)
    for body, x_ref, y_ref in zip(side_bodies, side_in, side_out):
        body(x_ref, y_ref, step)

    acc = jnp.dot(h_ref[...], w_scr[...], preferred_element_type=_F32)
    o_ref[...] = _ACTS[act](acc).astype(o_ref.dtype)


def _proj(h, w_in, layer, col0, ncols, act, name, sides=(), tm=PROJ_TM):
    m = h.shape[0]
    jb0 = col0 // PROJ_TN
    n_i = m // tm
    grid = (ncols // PROJ_TN, n_i)
    slab = lambda j, i: (j * n_i + i, 0)
    for s in sides:
        assert s.rows * grid[0] * grid[1] == s.out_rows, (name, s.rows, s.out_rows)
    outs = pl.pallas_call(
        partial(_proj_kernel, act=act, side_bodies=tuple(s.body for s in sides)),
        grid=grid,
        in_specs=[
            pl.BlockSpec((tm, D_MODEL), lambda j, i: (i, 0)),
            pl.BlockSpec((None, D_MODEL, PROJ_TN), lambda j, i: (layer, 0, jb0 + j)),
        ] + [pl.BlockSpec((None, s.rows, s.array.shape[2]), lambda j, i: (layer, j * n_i + i, 0))
             for s in sides],
        out_specs=[pl.BlockSpec((tm, PROJ_TN), lambda j, i: (i, j))]
        + [pl.BlockSpec((s.rows, s.out_cols), slab) for s in sides],
        out_shape=[jax.ShapeDtypeStruct((m, ncols), _BF16)]
        + [jax.ShapeDtypeStruct((s.out_rows, s.out_cols), _BF16) for s in sides],
        scratch_shapes=[pltpu.VMEM((D_MODEL, PROJ_TN), _BF16)],
        compiler_params=_params(("arbitrary", "arbitrary")),
        name=name,
    )(h, w_in, *[s.array for s in sides])
    return outs


def _gmlp_gate(uv_ref, lng_ref, lnb_ref, ws_ref, bs_ref, ya_ref):
    v = uv_ref[:, GM_WIDTH:].astype(_F32)
    mu = jnp.mean(v, axis=-1, keepdims=True)
    vc = v - mu
    var = jnp.mean(vc * vc, axis=-1, keepdims=True)
    vn = (vc * lax.rsqrt(var + LN_EPS) * lng_ref[...] + lnb_ref[...]).astype(_BF16)
    for n in range(uv_ref.shape[0] // CHUNK):
        rows = slice(n * CHUNK, (n + 1) * CHUNK)
        for g in range(GM_GROUPS):
            cols = slice(g * GM_HEAD, (g + 1) * GM_HEAD)
            mixed = jnp.dot(ws_ref[g], vn[rows, cols], preferred_element_type=_F32) + bs_ref[g]
            ya_ref[rows, cols] = (uv_ref[rows, cols].astype(_F32) * mixed).astype(ya_ref.dtype)


RPB_LANE0 = LANES // 2 - (NA_KC - 1)


def _prologue_kernel(rpb_ref, x_ref, g_ref, b_ref, h_ref):
    h_ref[...] = _rms(x_ref[...], g_ref[...]).astype(h_ref.dtype)

    c = lax.broadcasted_iota(jnp.int32, (GRID_W, LANES), 0)
    lane = lax.broadcasted_iota(jnp.int32, (GRID_W, LANES), 1)
    first = lane < GRID_W
    k = jnp.where(first, lane, lane - GRID_W)
    cs = jnp.clip(c - NA_KC // 2, 0, GRID_W - NA_KC)
    in_win = (k >= cs) & (k < cs + NA_KC)

    def toeplitz(dr, shift):
        row = jnp.broadcast_to(rpb_ref[0, dr:dr + 1, :], (GRID_W, LANES))
        return pltpu.roll(row, shift, axis=1, stride=1, stride_axis=0)

    n_dr = 2 * NA_KR - 1
    pairs = [jnp.where(in_win, jnp.where(first, toeplitz(dr, LANES // 2), toeplitz(dr + 1, 0)),
                       NEG_INF) for dr in range(n_dr - 1)]
    for cls, r in enumerate(NA_CLASS_ROW):
        rs = min(max(r - NA_KR // 2, 0), ROWS - NA_KR)
        for i in range(0, NA_KR, 2):
            b_ref[0, cls, :, i * GRID_W:(i + 2) * GRID_W] = pairs[rs + i - r + NA_KR - 1]


def _prologue(rpb, x, g):
    n = DEPTH * NA_HEADS
    m = x.shape[0]
    rows = m // n
    rpb_p = jnp.pad(rpb.reshape(n, 2 * NA_KR - 1, 2 * NA_KC - 1),
                    ((0, 0), (0, 1), (RPB_LANE0, LANES - RPB_LANE0 - (2 * NA_KC - 1))))
    return pl.pallas_call(
        _prologue_kernel,
        grid=(n,),
        in_specs=[pl.BlockSpec((1, 2 * NA_KR, LANES), lambda s: (s, 0, 0)),
                  pl.BlockSpec((rows, D_MODEL), lambda s: (s, 0)),
                  pl.BlockSpec((1, D_MODEL), lambda s: (0, 0))],
        out_specs=[pl.BlockSpec((1, N_CLASSES, GRID_W, NA_KTOK), lambda s: (s // 2, 0, s % 2, 0)),
                   pl.BlockSpec((rows, D_MODEL), lambda s: (s, 0))],
        out_shape=[jax.ShapeDtypeStruct((n // 2, N_CLASSES, 2 * GRID_W, NA_KTOK), _F32),
                   jax.ShapeDtypeStruct((m, D_MODEL), _BF16)],
        compiler_params=_params(("arbitrary",)),
        name="prologue",
    )(rpb_p, x, g)


def _na_window(r):
    rs = jnp.clip(r - NA_KR // 2, 0, ROWS - NA_KR)
    cls = jnp.where(r < NA_INTERIOR_CLASS, r,
                    jnp.where(r > ROWS - NA_KR // 2, r - (ROWS - NA_KR), NA_INTERIOR_CLASS))
    return pl.ds(pl.multiple_of(rs * GRID_W, GRID_W), NA_KTOK), cls


def _na_kernel(q_ref, k_ref, v_ref, b_ref, o_ref):
    rb = pl.program_id(1)
    scale = NA_HEAD_DIM ** -0.5
    head0 = lax.broadcasted_iota(jnp.int32, (GRID_W, LANES), 1) < NA_HEAD_DIM

    def apply(t, p, inv):
        qrows = slice(t * GRID_W, (t + 1) * GRID_W)
        vrows, _ = _na_window(rb * NA_ROWS_PER_STEP + t)
        o = jnp.dot(p, v_ref[vrows, :], preferred_element_type=_F32) * inv
        o_ref[qrows, :] = jnp.where(head0, o[:GRID_W], o[GRID_W:]).astype(o_ref.dtype)

    pending = []
    for t in range(NA_ROWS_PER_STEP):
        krows, cls = _na_window(rb * NA_ROWS_PER_STEP + t)
        q = q_ref[t * GRID_W:(t + 1) * GRID_W, :] * scale
        zero = jnp.zeros_like(q)
        q2 = jnp.concatenate([jnp.where(head0, q, zero), jnp.where(head0, zero, q)], axis=0)
        s = lax.dot_general(q2, k_ref[krows, :], (((1,), (1,)), ((), ())),
                            preferred_element_type=_F32) + b_ref[0, cls]
        p = jnp.exp(s - jnp.max(s, axis=-1, keepdims=True))
        den = jnp.sum(p, axis=-1, keepdims=True)
        pending.append((t, p.astype(_BF16), 1.0 / den))
        if len(pending) > NA_PV_LAG:
            apply(*pending.pop(0))
    for item in pending:
        apply(*item)


def _na(qkv, bias, layer):
    m = qkv.shape[0]
    tq = NA_ROWS_PER_STEP * GRID_W
    n_hp = NA_WIDTH // LANES
    return pl.pallas_call(
        _na_kernel,
        grid=(n_hp, m // tq),
        in_specs=[
            pl.BlockSpec((tq, LANES), lambda hp, rb: (rb, hp)),
            pl.BlockSpec((m, LANES), lambda hp, rb: (0, n_hp + hp)),
            pl.BlockSpec((m, LANES), lambda hp, rb: (0, 2 * n_hp + hp)),
            pl.BlockSpec((1, N_CLASSES, 2 * GRID_W, NA_KTOK),
                         lambda hp, rb: (layer * n_hp + hp, 0, 0, 0)),
        ],
        out_specs=pl.BlockSpec((tq, LANES), lambda hp, rb: (rb, hp)),
        out_shape=jax.ShapeDtypeStruct((m, NA_WIDTH), _BF16),
        compiler_params=_params(("arbitrary", "arbitrary")),
        name="na_attn",
    )(qkv, qkv, qkv, bias)


def _merge_kernel(x_ref, uv_ref, lng_ref, lnb_ref, ws_ref, bs_ref, yb_ref, gt_ref, wb_ref, wo_ref,
                  g_ref, o_ref, h_ref, ya_scr):
    _gmlp_gate(uv_ref, lng_ref, lnb_ref, ws_ref, bs_ref, ya_scr)
    ta = jnp.dot(ya_scr[...], wb_ref[:GM_WIDTH, :], preferred_element_type=_F32)
    tb = jnp.dot(yb_ref[...], wb_ref[GM_WIDTH:, :], preferred_element_type=_F32)
    merged = (gt_ref[:, :D_MODEL].astype(_F32) * ta + gt_ref[:, D_MODEL:].astype(_F32) * tb)
    y = x_ref[...] + jnp.dot(merged.astype(_BF16), wo_ref[...], preferred_element_type=_F32)
    o_ref[...] = y
    h_ref[...] = _rms(y, g_ref[...]).astype(h_ref.dtype)


def _merge(x, uv, ln_g, ln_b, w_s, b_s, yb, gates, w_branch, w_out, g_ffn):
    m = x.shape[0]
    resident = dict(pipeline_mode=pl.Buffered(1))
    row = lambda i: (i, 0)
    fixed2 = lambda i: (0, 0)
    fixed3 = lambda i: (0, 0, 0)
    return pl.pallas_call(
        _merge_kernel,
        grid=(m // MERGE_TM,),
        in_specs=[
            pl.BlockSpec((MERGE_TM, D_MODEL), row),
            pl.BlockSpec((MERGE_TM, 2 * GM_WIDTH), row),
            pl.BlockSpec((1, GM_WIDTH), fixed2),
            pl.BlockSpec((1, GM_WIDTH), fixed2),
            pl.BlockSpec((GM_GROUPS, CHUNK, CHUNK), fixed3),
            pl.BlockSpec((GM_GROUPS, CHUNK, 1), fixed3),
            pl.BlockSpec((MERGE_TM, NA_WIDTH), row),
            pl.BlockSpec((MERGE_TM, 2 * D_MODEL), row),
            pl.BlockSpec((GM_WIDTH + NA_WIDTH, D_MODEL), fixed2, **resident),
            pl.BlockSpec((D_MODEL, D_MODEL), fixed2, **resident),
            pl.BlockSpec((1, D_MODEL), fixed2),
        ],
        out_specs=[pl.BlockSpec((MERGE_TM, D_MODEL), row), pl.BlockSpec((MERGE_TM, D_MODEL), row)],
        out_shape=[jax.ShapeDtypeStruct((m, D_MODEL), _F32),
                   jax.ShapeDtypeStruct((m, D_MODEL), _BF16)],
        scratch_shapes=[pltpu.VMEM((MERGE_TM, GM_WIDTH), _BF16)],
        compiler_params=_params(("arbitrary",)),
        name="merge_out",
    )(x, uv, ln_g, ln_b, w_s, b_s, yb, gates, w_branch, w_out, g_ffn)


def _ffn_kernel(x_hbm, h_ref, hp_ref, hn_ref, wg_ref, wv_ref, kg_ref, kv_ref, bg_ref, bv_ref,
                wd_ref, gn_ref, *refs, final_norm):
    if final_norm:
        o_hbm, h_scr, ug_scr, uv_scr, act_scr, acc_scr, x_sem, o_sem = refs
    else:
        o_hbm, ho_ref, h_scr, ug_scr, uv_scr, act_scr, acc_scr, x_sem, o_sem = refs
    i, j = pl.program_id(0), pl.program_id(1)
    tm = FFN_TM
    n_row_tiles = o_hbm.shape[0] // tm
    o_ref = acc_scr.at[i % 2]

    def tile_rows(tile):
        return pl.ds(pl.multiple_of(tile * tm, tm), tm)

    def x_copy():
        return pltpu.make_async_copy(x_hbm.at[tile_rows(i), :], o_ref, x_sem)

    def o_copy(tile):
        return pltpu.make_async_copy(acc_scr.at[tile % 2], o_hbm.at[tile_rows(tile), :],
                                     o_sem.at[tile % 2])

    def up_dots():
        h = h_scr[...]
        for u_scr, w_ref in ((ug_scr, wg_ref), (uv_scr, wv_ref)):
            u = jnp.dot(h, w_ref[...], preferred_element_type=_F32)
            for c in range(FF_TILE // LANES):
                u_scr[c] = u[:, c * LANES:(c + 1) * LANES]

    def conv(u_scr, k_ref, b_ref, c):
        cols = slice(c * LANES, (c + 1) * LANES)
        return (u_scr[c, HALO - 1:HALO - 1 + tm, :] * k_ref[0:1, cols]
                + u_scr[c, HALO:HALO + tm, :] * k_ref[1:2, cols]
                + u_scr[c, HALO + 1:HALO + 1 + tm, :] * k_ref[2:3, cols] + b_ref[:, cols])

    def gate_into(slot):
        for c in range(FF_TILE // LANES):
            gate = conv(ug_scr, kg_ref, bg_ref, c)
            val = conv(uv_scr, kv_ref, bv_ref, c)
            act_scr[slot, :, c * LANES:(c + 1) * LANES] = (
                gate * _sigmoid(gate) * val).astype(_BF16)

    def down_from(slot):
        o_ref[...] += jnp.dot(act_scr[slot], wd_ref[...], preferred_element_type=_F32)

    @pl.when(j == 0)
    def _():
        zero = jnp.zeros((HALO, D_MODEL), _BF16)
        h_scr[:HALO, :] = jnp.where(i > 0, hp_ref[...], zero)
        h_scr[HALO:HALO + tm, :] = h_ref[...]
        h_scr[HALO + tm:, :] = jnp.where(i < pl.num_programs(0) - 1, hn_ref[...], zero)

    slots = FFN_LAG + 1
    for s in range(FFN_LAG):
        @pl.when(j == s)
        def _(s=s):
            if s == FFN_LAG - 1:
                @pl.when(i >= 2)
                def _():
                    o_copy(i - 2).wait()
                x_copy().start()
            up_dots()
            gate_into(s)

    @pl.when(j == FFN_LAG)
    def _():
        x_copy().wait()

    for r in range(slots):
        @pl.when((j >= FFN_LAG) & (j < N_FF_TILES) & (j % slots == r))
        def _(r=r):
            up_dots()
            down_from((r - FFN_LAG) % slots)
            gate_into(r)

    for s in range(N_FF_TILES, N_FF_TILES + FFN_LAG):
        @pl.when(j == s)
        def _(s=s):
            down_from((s - FFN_LAG) % slots)
            if s == N_FF_TILES + FFN_LAG - 1:
                hn = _rms(o_ref[...], gn_ref[...])
                if final_norm:
                    o_ref[...] = hn
                else:
                    ho_ref[...] = hn.astype(ho_ref.dtype)
                o_copy(i).start()

                @pl.when(i == n_row_tiles - 1)
                def _():
                    if n_row_tiles > 1:
                        o_copy(i - 1).wait()
                    o_copy(i).wait()


def _ffn(x, h, w_up, conv_k, conv_b, w_down, layer, g_next, final_norm):
    m = x.shape[0]
    tm = FFN_TM
    nb = tm // HALO
    last = m // HALO - 1
    row = lambda i, j: (i, 0)
    up_tile = lambda j: jnp.minimum(j, N_FF_TILES - 1)
    gate = lambda i, j: (layer, 0, up_tile(j))
    val = lambda i, j: (layer, 0, N_FF_TILES + up_tile(j))
    out_specs = [pl.BlockSpec(memory_space=pl.ANY)]
    out_shape = [jax.ShapeDtypeStruct((m, D_MODEL), _F32)]
    if not final_norm:
        out_specs.append(pl.BlockSpec((tm, D_MODEL), row))
        out_shape.append(jax.ShapeDtypeStruct((m, D_MODEL), _BF16))
    return pl.pallas_call(
        partial(_ffn_kernel, final_norm=final_norm),
        grid=(m // tm, N_FF_TILES + FFN_LAG),
        in_specs=[
            pl.BlockSpec(memory_space=pl.ANY),
            pl.BlockSpec((tm, D_MODEL), row),
            pl.BlockSpec((HALO, D_MODEL), lambda i, j: (jnp.maximum(i * nb - 1, 0), 0)),
            pl.BlockSpec((HALO, D_MODEL), lambda i, j: (jnp.minimum((i + 1) * nb, last), 0)),
            pl.BlockSpec((D_MODEL, FF_TILE), lambda i, j: (0, up_tile(j))),
            pl.BlockSpec((D_MODEL, FF_TILE), lambda i, j: (0, N_FF_TILES + up_tile(j))),
            pl.BlockSpec((None, 3, FF_TILE), gate),
            pl.BlockSpec((None, 3, FF_TILE), val),
            pl.BlockSpec((None, 1, FF_TILE), gate),
            pl.BlockSpec((None, 1, FF_TILE), val),
            pl.BlockSpec((FF_TILE, D_MODEL),
                         lambda i, j: (jnp.clip(j - FFN_LAG, 0, N_FF_TILES - 1), 0)),
            pl.BlockSpec((1, D_MODEL), lambda i, j: (0, 0)),
        ],
        out_specs=out_specs,
        out_shape=out_shape,
        scratch_shapes=[
            pltpu.VMEM((tm + 2 * HALO, D_MODEL), _BF16),
            pltpu.VMEM((FF_TILE // LANES, tm + 2 * HALO, LANES), _F32),
            pltpu.VMEM((FF_TILE // LANES, tm + 2 * HALO, LANES), _F32),
            pltpu.VMEM((FFN_LAG + 1, tm, FF_TILE), _BF16),
            pltpu.VMEM((2, tm, D_MODEL), _F32),
            pltpu.SemaphoreType.DMA(()),
            pltpu.SemaphoreType.DMA((2,)),
        ],
        compiler_params=_params(("arbitrary", "arbitrary")),
        name="conv_ffn",
    )(x, h, h, h, w_up, w_up, conv_k, conv_k, conv_b, conv_b, w_down, g_next)


def _pad_ff(a):
    gate, val = jnp.split(a, 2, axis=-1)
    pad = [(0, 0)] * (a.ndim - 1) + [(0, D_FF_PAD - D_FF)]
    return jnp.concatenate([jnp.pad(gate, pad), jnp.pad(val, pad)], axis=-1)


def kernel(x, norm_mix, w_in, gm_ln_g, gm_ln_b, gm_w_s, gm_b_s, na_rpb, w_branch, w_out,
           norm_ffn, w_up, conv_k, conv_b, w_down, norm_final):
    b, t, d = x.shape
    assert (b, t, d) == (1, SEQ, D_MODEL)
    xs = x.reshape(t, d)
    bias, h = _prologue(na_rpb, xs, norm_mix[0].reshape(1, d))
    conv_k_p, conv_b_p = _pad_ff(conv_k), _pad_ff(conv_b.reshape(DEPTH, 1, -1))
    w_s_b = gm_w_s.astype(_BF16)
    uv0, qkv0, gt0 = 0, 2 * GM_WIDTH, 2 * GM_WIDTH + 3 * NA_WIDTH
    n_uv = (2 * GM_WIDTH // PROJ_TN) * (t // PROJ_TM)
    n_gt = (2 * D_MODEL // PROJ_TN) * (t // PROJ_TM)
    uv_sides = (
        SideCast(_cast_down_rows, w_down, D_FF_PAD // n_uv, D_FF_PAD, d),
        SideCast(_cast_rows, w_branch, w_branch.shape[1] // n_uv, w_branch.shape[1], d),
        SideCast(_cast_rows, w_out, d // n_uv, d, d),
    )
    gt_sides = (SideCast(_cast_up_rows, w_up, d // n_gt, d, 2 * D_FF_PAD),)
    for l in range(DEPTH):
        uv, w_down_b, w_branch_b, w_out_b = _proj(h, w_in, l, uv0, 2 * GM_WIDTH, "gelu",
                                                  "proj_uv", uv_sides)
        qkv, = _proj(h, w_in, l, qkv0, 3 * NA_WIDTH, "none", "proj_qkv", tm=PROJ_TM_PLAIN)
        gates, w_up_b = _proj(h, w_in, l, gt0, 2 * D_MODEL, "sigmoid", "proj_gates", gt_sides)
        yb = _na(qkv, bias, l)
        xs, h = _merge(xs, uv, gm_ln_g[l].reshape(1, -1), gm_ln_b[l].reshape(1, -1), w_s_b[l],
                       gm_b_s[l].reshape(GM_GROUPS, CHUNK, 1), yb, gates, w_branch_b, w_out_b,
                       norm_ffn[l].reshape(1, d))
        if l == DEPTH - 1:
            xs, = _ffn(xs, h, w_up_b, conv_k_p, conv_b_p, w_down_b, l,
                       norm_final.reshape(1, d), True)
        else:
            xs, h = _ffn(xs, h, w_up_b, conv_k_p, conv_b_p, w_down_b, l,
                         norm_mix[l + 1].reshape(1, d), False)
    return xs.reshape(b, t, d)
```
